```python
import jax, jax.numpy as jnp
from jax import lax
import numpy as np

D_MODEL = 1024
BATCH = 16
SEQ = 2048
DEPTH = 2

D_FF = 2816
HG_HEADS = 8
HG_DK = 128
HG_DV = D_MODEL // HG_HEADS
HG_F = HG_HEADS * HG_DK
HG_V = HG_HEADS * HG_DV
HG_CHUNK = 64
FOX_HEADS = 16
FOX_DH = 64
FOX_DIM = FOX_HEADS * FOX_DH
Q_BLOCK = 128
LN_EPS = 1e-5
RMS_EPS = 1e-6

kernel_name = 'yoco_hgrn2_fox_macaron_deepnorm'


def _layer_norm(x, g, b):
    xf = x.astype(jnp.float32)
    mu = jnp.mean(xf, axis=-1, keepdims=True)
    var = jnp.mean(jnp.square(xf - mu), axis=-1, keepdims=True)
    return ((xf - mu) * lax.rsqrt(var + LN_EPS) * g + b).astype(x.dtype)


def _swiglu(x, w_gate_up, w_down):
    gate, up = jnp.split(x @ w_gate_up, 2, axis=-1)
    return (jax.nn.silu(gate) * up) @ w_down


def _hgrn2(x, w_in, lb, norm_g, w_out):
    b, t, _ = x.shape
    nc = t // HG_CHUNK
    proj = x @ w_in
    q, f_logit, i, g = jnp.split(proj, [HG_F, 2 * HG_F, 2 * HG_F + HG_V], axis=-1)
    f = lb + (1.0 - lb) * jax.nn.sigmoid(f_logit.astype(jnp.float32))
    log_f = jnp.log(f)
    k = 1.0 - f
    q = jax.nn.silu(q.astype(jnp.float32))

    def chunks(a, d):
        return a.reshape(b, nc, HG_CHUNK, HG_HEADS, d).transpose(1, 0, 3, 2, 4)

    qc, kc, gc = chunks(q, HG_DK), chunks(k, HG_DK), chunks(log_f, HG_DK)
    vc = chunks(i.astype(jnp.float32), HG_DV)
    causal = jnp.tril(jnp.ones((HG_CHUNK, HG_CHUNK), dtype=bool))

    def step(state, inp):
        qi, ki, vi, gi = inp
        G = jnp.cumsum(gi, axis=2)
        diff = G[:, :, :, None, :] - G[:, :, None, :, :]
        decay = jnp.exp(jnp.where(causal[:, :, None], diff, -jnp.inf))
        scores = jnp.einsum('bhtd,bhsd,bhtsd->bhts', qi, ki, decay)
        out = jnp.einsum('bhts,bhsv->bhtv', scores, vi) + jnp.einsum('bhtd,bhdv->bhtv', qi * jnp.exp(G), state)
        g_end = G[:, :, -1, :]
        state = jnp.exp(g_end)[..., None] * state + jnp.einsum(
            'bhsd,bhsv->bhdv', ki * jnp.exp(g_end[:, :, None, :] - G), vi)
        return state, out

    s0 = jnp.zeros((b, HG_HEADS, HG_DK, HG_DV), jnp.float32)
    _, o = lax.scan(step, s0, (qc, kc, vc, gc))
    o = o.transpose(1, 0, 3, 2, 4).reshape(b, t, HG_HEADS, HG_DV)
    o = o * lax.rsqrt(jnp.mean(o * o, axis=-1, keepdims=True) + RMS_EPS) * norm_g
    o = o.reshape(b, t, HG_V) * jax.nn.silu(g.astype(jnp.float32))
    return o.astype(x.dtype) @ w_out


def _shared_kv(h, kv_w, fg_w, fg_b):
    b, t, _ = h.shape
    k, v = jnp.split(h @ kv_w, 2, axis=-1)
    k = k.reshape(b, t, FOX_HEADS, FOX_DH)
    v = v.reshape(b, t, FOX_HEADS, FOX_DH)
    log_f = jax.nn.log_sigmoid((h @ fg_w + fg_b).astype(jnp.float32))
    c = jnp.cumsum(log_f, axis=1).transpose(0, 2, 1)
    return k, v, c


def _fox(x, k, v, c, w_q, w_out):
    b, t, _ = x.shape
    nb = t // Q_BLOCK
    scale = FOX_DH ** -0.5
    q = (x @ w_q).reshape(b, nb, Q_BLOCK, FOX_HEADS, FOX_DH).transpose(1, 0, 2, 3, 4)
    cq = c.reshape(b, FOX_HEADS, nb, Q_BLOCK).transpose(2, 0, 1, 3)
    pos_k = jnp.arange(t)

    def block(args):
        qi, ci, bi = args
        s = jnp.einsum('bqhd,bkhd->bhqk', qi, k).astype(jnp.float32) * scale
        s = s + (ci[..., None] - c[:, :, None, :])
        pos_q = bi * Q_BLOCK + jnp.arange(Q_BLOCK)
        s = jnp.where(pos_k[None, :] <= pos_q[:, None], s, -jnp.inf)
        p = jax.nn.softmax(s, axis=-1)
        return jnp.einsum('bhqk,bkhd->bqhd', p.astype(v.dtype), v)

    o = lax.map(block, (q, cq, jnp.arange(nb)))
    o = o.transpose(1, 0, 2, 3, 4).reshape(b, t, FOX_DIM)
    return o @ w_out


def setup_inputs(seed: int = 0) -> dict:
    key = jax.random.key(seed)
    ks = jax.random.split(key, 18)
    n_a = DEPTH // 2
    n_b = DEPTH - n_a
    beta = (8.0 * DEPTH) ** -0.25
    s = D_MODEL ** -0.5
    nrm = jax.random.normal
    f32 = jnp.float32
    return {
        'x': nrm(ks[0], (BATCH, SEQ, D_MODEL), f32),
        'ffn_ln_g': 1.0 + 0.02 * nrm(ks[1], (DEPTH, 2, D_MODEL), f32),
        'ffn_ln_b': 0.02 * nrm(ks[2], (DEPTH, 2, D_MODEL), f32),
        'ffn_w_gate_up': nrm(ks[3], (DEPTH, 2, D_MODEL, 2 * D_FF), f32) * s,
        'ffn_w_down': nrm(ks[4], (DEPTH, 2, D_FF, D_MODEL), f32) * (D_FF ** -0.5) * beta,
        'mix_ln_g': 1.0 + 0.02 * nrm(ks[5], (DEPTH, D_MODEL), f32),
        'mix_ln_b': 0.02 * nrm(ks[6], (DEPTH, D_MODEL), f32),
        'hg_w_in': nrm(ks[7], (n_a, D_MODEL, 2 * HG_F + 2 * HG_V), f32) * s,
        'hg_lower_bounds': nrm(ks[8], (n_a + 1, HG_F), f32),
        'hg_norm_g': 1.0 + 0.02 * nrm(ks[9], (n_a, HG_DV), f32),
        'hg_w_out': nrm(ks[10], (n_a, HG_V, D_MODEL), f32) * (HG_V ** -0.5) * beta,
        'kv_w': nrm(ks[11], (D_MODEL, 2 * FOX_DIM), f32) * s,
        'kv_fg_w': nrm(ks[12], (D_MODEL, FOX_HEADS), f32) * s,
        'kv_fg_b': 1.0 + 0.1 * nrm(ks[13], (FOX_HEADS,), f32),
        'fox_w_q': nrm(ks[14], (n_b, D_MODEL, FOX_DIM), f32) * s,
        'fox_w_out': nrm(ks[15], (n_b, FOX_DIM, D_MODEL), f32) * (FOX_DIM ** -0.5) * beta,
    }


def reference(x, ffn_ln_g, ffn_ln_b, ffn_w_gate_up, ffn_w_down, mix_ln_g, mix_ln_b,
              hg_w_in, hg_lower_bounds, hg_norm_g, hg_w_out,
              kv_w, kv_fg_w, kv_fg_b, fox_w_q, fox_w_out):
    n_a = DEPTH // 2
    alpha = (2.0 * DEPTH) ** 0.25
    lbs = jnp.cumsum(jax.nn.softmax(hg_lower_bounds.astype(jnp.float32), axis=0), axis=0)
    shared = None
    for l in range(DEPTH):
        x = _layer_norm(alpha * x + 0.5 * _swiglu(x, ffn_w_gate_up[l, 0], ffn_w_down[l, 0]),
                        ffn_ln_g[l, 0], ffn_ln_b[l, 0])
        if l < n_a:
            y = _hgrn2(x, hg_w_in[l], lbs[l], hg_norm_g[l], hg_w_out[l])
        else:
            k, v, c = shared
            y = _fox(x, k, v, c, fox_w_q[l - n_a], fox_w_out[l - n_a])
        x = _layer_norm(alpha * x + y, mix_ln_g[l], mix_ln_b[l])
        x = _layer_norm(alpha * x + 0.5 * _swiglu(x, ffn_w_gate_up[l, 1], ffn_w_down[l, 1]),
                        ffn_ln_g[l, 1], ffn_ln_b[l, 1])
        if l == n_a - 1:
            shared = _shared_kv(x, kv_w, kv_fg_w, kv_fg_b)
    return x
```

```python
import functools

import numpy as np
import jax
import jax.numpy as jnp
from jax import lax
from jax.experimental import pallas as pl
from jax.experimental.pallas import tpu as pltpu

F32 = jnp.float32
BF16 = jnp.bfloat16

LN_EPS = 1e-5
RMS_EPS = 1e-6
HG_CHUNK = 64
VMEM_LIMIT_BYTES = 56 * 1024 * 1024
NEG_BIG = -1e30


def _dot(a, b):
    return jnp.dot(a, b, preferred_element_type=F32)


def _dot_nt(a, b):
    return lax.dot_general(a, b, (((1,), (1,)), ((), ())), preferred_element_type=F32)


def _dot_tn(a, b):
    return lax.dot_general(a, b, (((0,), (0,)), ((), ())), preferred_element_type=F32)


def _sigmoid(x):
    return 1.0 / (1.0 + jnp.exp(-x))


def _silu(x):
    return x * _sigmoid(x)


def _layer_norm(y, g, b):
    mu = jnp.mean(y, axis=-1, keepdims=True)
    d = y - mu
    var = jnp.mean(d * d, axis=-1, keepdims=True)
    return d * lax.rsqrt(var + LN_EPS) * g + b


def _resident(shape):
    nd = len(shape)
    return pl.BlockSpec(shape, lambda *_: (0,) * nd, pipeline_mode=pl.Buffered(1))


def _ffn_kernel(x_ref, wg_ref, wu_ref, wd_ref, g_ref, b_ref, o_ref, xb_ref, acc_ref,
                *, alpha, n_chunks):
    xb_ref[...] = x_ref[...].astype(BF16)
    acc_ref[...] = jnp.zeros_like(acc_ref)

    def body(c, carry):
        xb = xb_ref[...]
        gate = _dot(xb, wg_ref[c])
        up = _dot(xb, wu_ref[c])
        h = (_silu(gate) * up).astype(BF16)
        acc_ref[...] += _dot(h, wd_ref[c])
        return carry

    lax.fori_loop(0, n_chunks, body, 0)
    y = alpha * x_ref[...] + 0.5 * acc_ref[...]
    o_ref[...] = _layer_norm(y, g_ref[...], b_ref[...])


def _ffn(x2, w_gate_up, w_down, g, b, *, alpha, tm=512, fc=256):
    n, d = x2.shape
    f = w_down.shape[0]
    nc = f // fc
    assert nc * fc == f and n % tm == 0
    wg = w_gate_up[:, :f].astype(BF16).reshape(d, nc, fc).transpose(1, 0, 2)
    wu = w_gate_up[:, f:].astype(BF16).reshape(d, nc, fc).transpose(1, 0, 2)
    wd = w_down.astype(BF16).reshape(nc, fc, d)
    kern = functools.partial(_ffn_kernel, alpha=alpha, n_chunks=nc)
    return pl.pallas_call(
        kern,
        grid=(n // tm,),
        in_specs=[
            pl.BlockSpec((tm, d), lambda i: (i, 0)),
            _resident((nc, d, fc)),
            _resident((nc, d, fc)),
            _resident((nc, fc, d)),
            _resident((1, d)),
            _resident((1, d)),
        ],
        out_specs=pl.BlockSpec((tm, d), lambda i: (i, 0)),
        out_shape=jax.ShapeDtypeStruct((n, d), F32),
        scratch_shapes=[pltpu.VMEM((tm, d), BF16), pltpu.VMEM((tm, d), F32)],
        compiler_params=pltpu.CompilerParams(
            dimension_semantics=("parallel",), vmem_limit_bytes=VMEM_LIMIT_BYTES),
        name="ffn",
    )(x2, wg, wu, wd, g.reshape(1, d), b.reshape(1, d))


def _hgrn2_tables(c):
    levels = int(np.log2(c))
    r = np.arange(c)
    w = np.zeros((levels + 2, c, c), np.float32)
    um = np.zeros((levels, c, 128), np.float32)
    mask = np.zeros((levels + 1, c, c), np.float32)
    w[0] = (r[None, :] <= r[:, None])
    for l in range(levels):
        m = c >> l
        p = (r // m) * m
        a = p + m // 2 - 1
        upper = r >= p + m // 2
        j = r[None, :]
        wu_ = (j > a[:, None]) & (j <= r[:, None])
        wl_ = (j > r[:, None]) & (j <= a[:, None])
        w[1 + l] = np.where(upper[:, None], wu_, wl_)
        um[l] = upper[:, None]
        mask[l] = (r[:, None] // m) == (r[None, :] // m)
    mask[levels] = np.eye(c)
    w[levels + 1] = (r[None, :] > r[:, None])
    return w.reshape((levels + 2) * c, c), um, mask, levels


def _hgrn2_kernel(x_ref, win_ref, lbp_ref, ng_ref, wout_ref, lng_ref, lnb_ref,
                  w3_ref, um_ref, mask_ref,
                  o_ref,
                  q_ref, k_ref, lf_ref, v_ref, gt_ref, oc_ref, e_ref, st_ref,
                  *, alpha, layer, heads, dk, dv, levels, tb):
    c = HG_CHUNK
    hf = heads * dk
    hv = heads * dv

    @pl.when(pl.program_id(1) == 0)
    def _():
        st_ref[...] = jnp.zeros_like(st_ref)

    lbp = lbp_ref[...]
    ex = jnp.exp(lbp - jnp.max(lbp, axis=0, keepdims=True))
    lb = jnp.sum(ex[:layer + 1], axis=0, keepdims=True) / jnp.sum(ex, axis=0, keepdims=True)

    xb = x_ref[...].astype(BF16)
    q_ref[...] = _silu(_dot(xb, win_ref[:, 0:hf]))
    f = lb + (1.0 - lb) * _sigmoid(_dot(xb, win_ref[:, hf:2 * hf]))
    lf_ref[...] = jnp.log(f)
    k_ref[...] = 1.0 - f
    v_ref[...] = _dot(xb, win_ref[:, 2 * hf:2 * hf + hv]).astype(BF16)
    gt_ref[...] = _silu(_dot(xb, win_ref[:, 2 * hf + hv:2 * hf + 2 * hv]))

    def chunk(ci, carry):
        rows = pl.ds(pl.multiple_of(ci * c, c), c)
        lf = lf_ref[rows, :]
        hi = lf.astype(BF16)
        r1 = lf - hi.astype(F32)
        mid = r1.astype(BF16)
        lo = (r1 - mid.astype(F32)).astype(BF16)
        lf3 = jnp.concatenate([hi, mid, lo], axis=0)
        e_ref[...] = _dot(w3_ref[...], lf3)
        for h in range(heads):
            ck = slice(h * dk, (h + 1) * dk)
            cv = slice(h * dv, (h + 1) * dv)
            qh = q_ref[rows, ck]
            kh = k_ref[rows, ck]
            vh = v_ref[rows, cv]
            a = None
            for l in range(levels):
                p = jnp.exp(e_ref[(1 + l) * c:(2 + l) * c, ck])
                pu = p * um_ref[l]
                ql = (qh * pu).astype(BF16)
                kl = (kh * (p - pu)).astype(BF16)
                s = _dot_nt(ql, kl)
                a = s if l == 0 else a + s * mask_ref[l]
            a = a + _dot_nt(qh.astype(BF16), kh.astype(BF16)) * mask_ref[levels]
            g = e_ref[0:c, ck]
            qi = (qh * jnp.exp(g)).astype(BF16)
            st = st_ref[h]
            o = _dot(a.astype(BF16), vh) + _dot_nt(qi, st.astype(BF16))
            oc_ref[rows, cv] = o
            ks = (kh * jnp.exp(e_ref[(levels + 1) * c:(levels + 2) * c, ck])).astype(BF16)
            g_end = e_ref[c - 1:c, ck]
            st_ref[h] = st * jnp.exp(g_end) + _dot_tn(vh, ks)
        return carry

    lax.fori_loop(0, tb // c, chunk, 0)

    ng = ng_ref[...]
    for h in range(heads):
        cv = slice(h * dv, (h + 1) * dv)
        oh = oc_ref[:, cv]
        ms = jnp.mean(oh * oh, axis=-1, keepdims=True)
        oc_ref[:, cv] = oh * lax.rsqrt(ms + RMS_EPS) * ng
    og = (oc_ref[...] * gt_ref[...]).astype(BF16)
    y = alpha * x_ref[...] + _dot(og, wout_ref[...])
    o_ref[...] = _layer_norm(y, lng_ref[...], lnb_ref[...])


def _hgrn2_layer(x3, w_in, lower_bounds, norm_g, w_out, ln_g, ln_b, *, alpha, layer, tb=256):
    b, t, d = x3.shape
    dv = norm_g.shape[-1]
    heads = w_out.shape[0] // dv
    hf = lower_bounds.shape[-1]
    dk = hf // heads
    hv = heads * dv
    assert dk == 128 and dv == 128 and t % tb == 0 and tb % HG_CHUNK == 0
    nt = t // tb
    w, um, mask, levels = _hgrn2_tables(HG_CHUNK)
    w3 = jnp.asarray(np.concatenate([w, w, w], axis=1), BF16)
    nr = lower_bounds.shape[0]
    kern = functools.partial(_hgrn2_kernel, alpha=alpha, layer=layer, heads=heads, dk=dk, dv=dv,
                             levels=levels, tb=tb)
    c = HG_CHUNK
    out = pl.pallas_call(
        kern,
        grid=(b, nt),
        in_specs=[
            pl.BlockSpec((tb, d), lambda i, j: (i * nt + j, 0)),
            _resident((d, 2 * hf + 2 * hv)),
            _resident((nr, hf)),
            _resident((1, dv)),
            _resident((hv, d)),
            _resident((1, d)),
            _resident((1, d)),
            _resident(w3.shape),
            _resident(um.shape),
            _resident(mask.shape),
        ],
        out_specs=pl.BlockSpec((tb, d), lambda i, j: (i * nt + j, 0)),
        out_shape=jax.ShapeDtypeStruct((b * t, d), F32),
        scratch_shapes=[
            pltpu.VMEM((tb, hf), F32),
            pltpu.VMEM((tb, hf), F32),
            pltpu.VMEM((tb, hf), F32),
            pltpu.VMEM((tb, hv), BF16),
            pltpu.VMEM((tb, hv), F32),
            pltpu.VMEM((tb, hv), F32),
            pltpu.VMEM(((levels + 2) * c, hf), F32),
            pltpu.VMEM((heads, dv, dk), F32),
        ],
        compiler_params=pltpu.CompilerParams(
            dimension_semantics=("parallel", "arbitrary"), vmem_limit_bytes=VMEM_LIMIT_BYTES),
        name="hgrn2",
    )(x3.reshape(b * t, d), w_in.astype(BF16), lower_bounds.astype(F32), norm_g.reshape(1, dv),
      w_out.astype(BF16), ln_g.reshape(1, d), ln_b.reshape(1, d),
      w3, jnp.asarray(um), jnp.asarray(mask))
    return out.reshape(b, t, d)


def _kv_kernel(h_ref, kvw_ref, fgw_ref, fgb_ref, tril_ref, k_ref, v_ref, c_ref, carry_ref, *, dim):
    @pl.when(pl.program_id(1) == 0)
    def _():
        carry_ref[...] = jnp.zeros_like(carry_ref)

    hb = h_ref[...].astype(BF16)
    k_ref[...] = _dot(hb, kvw_ref[:, 0:dim]).astype(BF16)
    v_ref[...] = _dot(hb, kvw_ref[:, dim:2 * dim]).astype(BF16)
    z = _dot(hb, fgw_ref[...]) + fgb_ref[...]
    lf = jnp.minimum(z, 0.0) - jnp.log(1.0 + jnp.exp(-jnp.abs(z)))
    hi = lf.astype(BF16)
    r1 = lf - hi.astype(F32)
    mid = r1.astype(BF16)
    lo = (r1 - mid.astype(F32)).astype(BF16)
    tril = tril_ref[...]
    cs = _dot(tril, hi) + _dot(tril, mid) + _dot(tril, lo) + carry_ref[...]
    c_ref[...] = cs
    carry_ref[...] = cs[cs.shape[0] - 1:, :]


def _shared_kv(h3, kv_w, fg_w, fg_b, *, tb=512):
    b, t, d = h3.shape
    dim = kv_w.shape[1] // 2
    heads = fg_w.shape[1]
    assert heads <= 128 and t % tb == 0
    nt = t // tb
    fgw = jnp.zeros((d, 128), BF16).at[:, :heads].set(fg_w.astype(BF16))
    fgb = jnp.zeros((1, 128), F32).at[0, :heads].set(fg_b.astype(F32))
    tril = jnp.asarray(np.tril(np.ones((tb, tb), np.float32)), BF16)
    kern = functools.partial(_kv_kernel, dim=dim)
    k, v, c = pl.pallas_call(
        kern,
        grid=(b, nt),
        in_specs=[
            pl.BlockSpec((tb, d), lambda i, j: (i * nt + j, 0)),
            _resident((d, 2 * dim)),
            _resident((d, 128)),
            _resident((1, 128)),
            _resident((tb, tb)),
        ],
        out_specs=[
            pl.BlockSpec((tb, dim), lambda i, j: (i * nt + j, 0)),
            pl.BlockSpec((tb, dim), lambda i, j: (i * nt + j, 0)),
            pl.BlockSpec((tb, 128), lambda i, j: (i * nt + j, 0)),
        ],
        out_shape=[
            jax.ShapeDtypeStruct((b * t, dim), BF16),
            jax.ShapeDtypeStruct((b * t, dim), BF16),
            jax.ShapeDtypeStruct((b * t, 128), F32),
        ],
        scratch_shapes=[pltpu.VMEM((1, 128), F32)],
        compiler_params=pltpu.CompilerParams(
            dimension_semantics=("parallel", "arbitrary"), vmem_limit_bytes=VMEM_LIMIT_BYTES),
        name="shared_kv",
    )(h3.reshape(b * t, d), kv_w.astype(BF16), fgw, fgb, tril)
    return k.reshape(b, t, dim), v.reshape(b, t, dim), c.reshape(b, t, 128)


def _fox_kernel(x_ref, wq_ref, k_ref, v_ref, cc_ref, cr_ref, wout_ref, lng_ref, lnb_ref,
                o_ref, q_ref, att_ref, *, alpha, heads, dh, tq):
    i = pl.program_id(1)
    scale = dh ** -0.5
    xb = x_ref[...].astype(BF16)
    q_ref[...] = (_dot(xb, wq_ref[...]) * scale).astype(BF16)
    lane = lax.broadcasted_iota(jnp.int32, (tq, 2 * dh), 1)
    low = lane < dh
    row = lax.broadcasted_iota(jnp.int32, (tq, tq), 0)
    col = lax.broadcasted_iota(jnp.int32, (tq, tq), 1)
    causal = col <= row

    for hp in range(heads // 2):
        cl = slice(hp * 2 * dh, (hp + 1) * 2 * dh)
        q2 = q_ref[:, cl]
        zero = jnp.zeros_like(q2)
        qa = jnp.where(low, q2, zero)
        qb = jnp.where(low, zero, q2)
        cca = cc_ref[:, 2 * hp:2 * hp + 1]
        ccb = cc_ref[:, 2 * hp + 1:2 * hp + 2]

        def step(j, state, masked):
            ma, la, acca, mb, lb_, accb = state
            rows = pl.ds(pl.multiple_of(j * tq, tq), tq)
            k2 = k_ref[0, rows, cl]
            v2 = v_ref[0, rows, cl]
            cra = cr_ref[0, 2 * hp, pl.ds(j, 1), :]
            crb = cr_ref[0, 2 * hp + 1, pl.ds(j, 1), :]

            def one(qx, ccx, crx, m, l, acc):
                s = _dot_nt(qx, k2) + (ccx - crx)
                if masked:
                    s = jnp.where(causal, s, NEG_BIG)
                m_new = jnp.maximum(m, jnp.max(s, axis=-1, keepdims=True))
                p = jnp.exp(s - m_new)
                corr = jnp.exp(m - m_new)
                l_new = corr * l + jnp.sum(p, axis=-1, keepdims=True)
                acc_new = corr * acc + _dot(p.astype(BF16), v2)
                return m_new, l_new, acc_new

            ma, la, acca = one(qa, cca, cra, ma, la, acca)
            mb, lb_, accb = one(qb, ccb, crb, mb, lb_, accb)
            return ma, la, acca, mb, lb_, accb

        init = (jnp.full((tq, 1), NEG_BIG, F32), jnp.zeros((tq, 1), F32),
                jnp.zeros((tq, 2 * dh), F32)) * 2
        state = lax.fori_loop(0, i, functools.partial(step, masked=False), init)
        ma, la, acca, mb, lb_, accb = step(i, state, True)
        att_ref[:, cl] = jnp.where(low, acca / la, accb / lb_).astype(BF16)

    y = alpha * x_ref[...] + _dot(att_ref[...], wout_ref[...])
    o_ref[...] = _layer_norm(y, lng_ref[...], lnb_ref[...])


def _fox_layer(x3, k, v, c, w_q, w_out, ln_g, ln_b, *, alpha, heads, tq=256):
    b, t, d = x3.shape
    dim = w_q.shape[1]
    dh = dim // heads
    assert 2 * dh == 128 and heads % 2 == 0 and t % tq == 0
    nt = t // tq
    c_row = c[:, :, :heads].transpose(0, 2, 1).reshape(b, heads, nt, tq)
    kern = functools.partial(_fox_kernel, alpha=alpha, heads=heads, dh=dh, tq=tq)
    out = pl.pallas_call(
        kern,
        grid=(b, nt),
        in_specs=[
            pl.BlockSpec((tq, d), lambda i, j: (i * nt + j, 0)),
            _resident((d, dim)),
            pl.BlockSpec((1, t, dim), lambda i, j: (i, 0, 0)),
            pl.BlockSpec((1, t, dim), lambda i, j: (i, 0, 0)),
            pl.BlockSpec((tq, 128), lambda i, j: (i * nt + j, 0)),
            pl.BlockSpec((1, heads, nt, tq), lambda i, j: (i, 0, 0, 0)),
            _resident((dim, d)),
            _resident((1, d)),
            _resident((1, d)),
        ],
        out_specs=pl.BlockSpec((tq, d), lambda i, j: (i * nt + j, 0)),
        out_shape=jax.ShapeDtypeStruct((b * t, d), F32),
        scratch_shapes=[pltpu.VMEM((tq, dim), BF16), pltpu.VMEM((tq, dim), BF16)],
        compiler_params=pltpu.CompilerParams(
            dimension_semantics=("parallel", "arbitrary"), vmem_limit_bytes=VMEM_LIMIT_BYTES),
        name="fox",
    )(x3.reshape(b * t, d), w_q.astype(BF16), k, v, c.reshape(b * t, 128), c_row,
      w_out.astype(BF16), ln_g.reshape(1, d), ln_b.reshape(1, d))
    return out.reshape(b, t, d)


def kernel(x, ffn_ln_g, ffn_ln_b, ffn_w_gate_up, ffn_w_down, mix_ln_g, mix_ln_b, hg_w_in, hg_lower_bounds, hg_norm_g, hg_w_out, kv_w, kv_fg_w, kv_fg_b, fox_w_q, fox_w_out):
    b, t, d = x.shape
    depth = ffn_w_gate_up.shape[0]
    n_a = hg_w_in.shape[0]
    alpha = (2.0 * depth) ** 0.25
    fox_heads = kv_fg_w.shape[1]

    def ffn(x3, l, s):
        return _ffn(x3.reshape(b * t, d), ffn_w_gate_up[l, s], ffn_w_down[l, s],
                    ffn_ln_g[l, s], ffn_ln_b[l, s], alpha=alpha).reshape(b, t, d)

    shared = None
    for l in range(depth):
        x = ffn(x, l, 0)
        if l < n_a:
            x = _hgrn2_layer(x, hg_w_in[l], hg_lower_bounds, hg_norm_g[l], hg_w_out[l],
                             mix_ln_g[l], mix_ln_b[l], alpha=alpha, layer=l)
        else:
            k, v, c = shared
            x = _fox_layer(x, k, v, c, fox_w_q[l - n_a], fox_w_out[l - n_a],
                           mix_ln_g[l], mix_ln_b[l], alpha=alpha, heads=fox_heads)
        x = ffn(x, l, 1)
        if l == n_a - 1:
            shared = _shared_kv(x, kv_w, kv_fg_w, kv_fg_b)
    return x
```

```python
import functools

import numpy as np
import jax
import jax.numpy as jnp
from jax import lax
from jax.experimental import pallas as pl
from jax.experimental.pallas import tpu as pltpu

F32 = jnp.float32
BF16 = jnp.bfloat16

LN_EPS = 1e-5
RMS_EPS = 1e-6
HG_CHUNK = 64
VMEM_LIMIT_BYTES = 56 * 1024 * 1024
NEG_BIG = -1e30
FOX_BLOCK = 256
FOX_AHEAD = 8
FOX_VROWS = 80
LOG2E = 1.4426950408889634


def _dot(a, b):
    return jnp.dot(a, b, preferred_element_type=F32)


def _dot_nt(a, b):
    return lax.dot_general(a, b, (((1,), (1,)), ((), ())), preferred_element_type=F32)


def _dot_tn(a, b):
    return lax.dot_general(a, b, (((0,), (0,)), ((), ())), preferred_element_type=F32)


def _sigmoid(x):
    return 1.0 / (1.0 + jnp.exp(-x))


def _silu(x):
    return x * _sigmoid(x)


def _layer_norm(y, g, b):
    mu = jnp.mean(y, axis=-1, keepdims=True)
    d = y - mu
    var = jnp.mean(d * d, axis=-1, keepdims=True)
    return d * lax.rsqrt(var + LN_EPS) * g + b


def _resident(shape):
    nd = len(shape)
    return pl.BlockSpec(shape, lambda *_: (0,) * nd, pipeline_mode=pl.Buffered(1))


def _split3(x):
    hi = x.astype(BF16)
    r1 = x - hi.astype(F32)
    mid = r1.astype(BF16)
    lo = (r1 - mid.astype(F32)).astype(BF16)
    return hi, mid, lo


def _ffn_kernel(x_ref, wg_ref, wu_ref, wd_ref, g_ref, b_ref, o_ref, xb_ref, acc_ref,
                *, alpha, n_chunks):
    xb_ref[...] = x_ref[...].astype(BF16)
    acc_ref[...] = jnp.zeros_like(acc_ref)

    def body(c, carry):
        xb = xb_ref[...]
        gate = _dot(xb, wg_ref[c])
        up = _dot(xb, wu_ref[c])
        h = (_silu(gate) * up).astype(BF16)
        acc_ref[...] += _dot(h, wd_ref[c])
        return carry

    lax.fori_loop(0, n_chunks, body, 0)
    y = alpha * x_ref[...] + 0.5 * acc_ref[...]
    o_ref[...] = _layer_norm(y, g_ref[...], b_ref[...])


def _ffn(x2, w_gate_up, w_down, g, b, *, alpha, tm=512, fc=256):
    n, d = x2.shape
    f = w_down.shape[0]
    nc = f // fc
    assert nc * fc == f and n % tm == 0
    wg = w_gate_up[:, :f].astype(BF16).reshape(d, nc, fc).transpose(1, 0, 2)
    wu = w_gate_up[:, f:].astype(BF16).reshape(d, nc, fc).transpose(1, 0, 2)
    wd = w_down.astype(BF16).reshape(nc, fc, d)
    kern = functools.partial(_ffn_kernel, alpha=alpha, n_chunks=nc)
    return pl.pallas_call(
        kern,
        grid=(n // tm,),
        in_specs=[
            pl.BlockSpec((tm, d), lambda i: (i, 0)),
            _resident((nc, d, fc)),
            _resident((nc, d, fc)),
            _resident((nc, fc, d)),
            _resident((1, d)),
            _resident((1, d)),
        ],
        out_specs=pl.BlockSpec((tm, d), lambda i: (i, 0)),
        out_shape=jax.ShapeDtypeStruct((n, d), F32),
        scratch_shapes=[pltpu.VMEM((tm, d), BF16), pltpu.VMEM((tm, d), F32)],
        compiler_params=pltpu.CompilerParams(
            dimension_semantics=("parallel",), vmem_limit_bytes=VMEM_LIMIT_BYTES),
        name="ffn",
    )(x2, wg, wu, wd, g.reshape(1, d), b.reshape(1, d))


def _hgrn2_tables(c):
    levels = int(np.log2(c))
    r = np.arange(c)
    w = np.zeros((levels + 2, c, c), np.float32)
    um = np.zeros((levels, c, 128), np.float32)
    mask = np.zeros((levels + 1, c, c), np.float32)
    w[0] = (r[None, :] <= r[:, None])
    for l in range(levels):
        m = c >> l
        p = (r // m) * m
        a = p + m // 2 - 1
        upper = r >= p + m // 2
        j = r[None, :]
        wu_ = (j > a[:, None]) & (j <= r[:, None])
        wl_ = (j > r[:, None]) & (j <= a[:, None])
        w[1 + l] = np.where(upper[:, None], wu_, wl_)
        um[l] = upper[:, None]
        mask[l] = (r[:, None] // m) == (r[None, :] // m)
    mask[levels] = np.eye(c)
    w[levels + 1] = (r[None, :] > r[:, None])
    return w.reshape((levels + 2) * c, c), um, mask, levels


def _hgrn2_kernel(x_ref, win_ref, lbp_ref, ng_ref, wout_ref, lng_ref, lnb_ref,
                  w3_ref, um_ref, mask_ref,
                  o_ref,
                  q_ref, k_ref, lf_ref, v_ref, gt_ref, oc_ref, e_ref, st_ref,
                  *, alpha, layer, heads, dk, dv, levels, tb):
    c = HG_CHUNK
    hf = heads * dk
    hv = heads * dv

    @pl.when(pl.program_id(1) == 0)
    def _():
        st_ref[...] = jnp.zeros_like(st_ref)

    lbp = lbp_ref[...]
    ex = jnp.exp(lbp - jnp.max(lbp, axis=0, keepdims=True))
    lb = jnp.sum(ex[:layer + 1], axis=0, keepdims=True) / jnp.sum(ex, axis=0, keepdims=True)

    xb = x_ref[...].astype(BF16)
    q_ref[...] = _silu(_dot(xb, win_ref[:, 0:hf]))
    f = lb + (1.0 - lb) * _sigmoid(_dot(xb, win_ref[:, hf:2 * hf]))
    lf_ref[...] = jnp.log(f)
    k_ref[...] = 1.0 - f
    v_ref[...] = _dot(xb, win_ref[:, 2 * hf:2 * hf + hv]).astype(BF16)
    gt_ref[...] = _silu(_dot(xb, win_ref[:, 2 * hf + hv:2 * hf + 2 * hv]))

    def chunk(ci, carry):
        rows = pl.ds(pl.multiple_of(ci * c, c), c)
        lf3 = jnp.concatenate(_split3(lf_ref[rows, :]), axis=0)
        e_ref[...] = _dot(w3_ref[...], lf3)
        for h in range(heads):
            ck = slice(h * dk, (h + 1) * dk)
            cv = slice(h * dv, (h + 1) * dv)
            qh = q_ref[rows, ck]
            kh = k_ref[rows, ck]
            vh = v_ref[rows, cv]
            a = None
            for l in range(levels):
                p = jnp.exp(e_ref[(1 + l) * c:(2 + l) * c, ck])
                pu = p * um_ref[l]
                ql = (qh * pu).astype(BF16)
                kl = (kh * (p - pu)).astype(BF16)
                s = _dot_nt(ql, kl)
                a = s if l == 0 else a + s * mask_ref[l]
            a = a + _dot_nt(qh.astype(BF16), kh.astype(BF16)) * mask_ref[levels]
            g = e_ref[0:c, ck]
            qi = (qh * jnp.exp(g)).astype(BF16)
            st = st_ref[h]
            o = _dot(a.astype(BF16), vh) + _dot_nt(qi, st.astype(BF16))
            oc_ref[rows, cv] = o
            ks = (kh * jnp.exp(e_ref[(levels + 1) * c:(levels + 2) * c, ck])).astype(BF16)
            g_end = e_ref[c - 1:c, ck]
            st_ref[h] = st * jnp.exp(g_end) + _dot_tn(vh, ks)
        return carry

    lax.fori_loop(0, tb // c, chunk, 0)

    ng = ng_ref[...]
    for h in range(heads):
        cv = slice(h * dv, (h + 1) * dv)
        oh = oc_ref[:, cv]
        ms = jnp.mean(oh * oh, axis=-1, keepdims=True)
        oc_ref[:, cv] = oh * lax.rsqrt(ms + RMS_EPS) * ng
    og = (oc_ref[...] * gt_ref[...]).astype(BF16)
    y = alpha * x_ref[...] + _dot(og, wout_ref[...])
    o_ref[...] = _layer_norm(y, lng_ref[...], lnb_ref[...])


def _hgrn2_layer(x3, w_in, lower_bounds, norm_g, w_out, ln_g, ln_b, *, alpha, layer, tb=256):
    b, t, d = x3.shape
    dv = norm_g.shape[-1]
    heads = w_out.shape[0] // dv
    hf = lower_bounds.shape[-1]
    dk = hf // heads
    hv = heads * dv
    assert dk == 128 and dv == 128 and t % tb == 0 and tb % HG_CHUNK == 0
    nt = t // tb
    w, um, mask, levels = _hgrn2_tables(HG_CHUNK)
    w3 = jnp.asarray(np.concatenate([w, w, w], axis=1), BF16)
    nr = lower_bounds.shape[0]
    kern = functools.partial(_hgrn2_kernel, alpha=alpha, layer=layer, heads=heads, dk=dk, dv=dv,
                             levels=levels, tb=tb)
    c = HG_CHUNK
    out = pl.pallas_call(
        kern,
        grid=(b, nt),
        in_specs=[
            pl.BlockSpec((tb, d), lambda i, j: (i * nt + j, 0)),
            _resident((d, 2 * hf + 2 * hv)),
            _resident((nr, hf)),
            _resident((1, dv)),
            _resident((hv, d)),
            _resident((1, d)),
            _resident((1, d)),
            _resident(w3.shape),
            _resident(um.shape),
            _resident(mask.shape),
        ],
        out_specs=pl.BlockSpec((tb, d), lambda i, j: (i * nt + j, 0)),
        out_shape=jax.ShapeDtypeStruct((b * t, d), F32),
        scratch_shapes=[
            pltpu.VMEM((tb, hf), F32),
            pltpu.VMEM((tb, hf), F32),
            pltpu.VMEM((tb, hf), F32),
            pltpu.VMEM((tb, hv), BF16),
            pltpu.VMEM((tb, hv), F32),
            pltpu.VMEM((tb, hv), F32),
            pltpu.VMEM(((levels + 2) * c, hf), F32),
            pltpu.VMEM((heads, dv, dk), F32),
        ],
        compiler_params=pltpu.CompilerParams(
            dimension_semantics=("parallel", "arbitrary"), vmem_limit_bytes=VMEM_LIMIT_BYTES),
        name="hgrn2",
    )(x3.reshape(b * t, d), w_in.astype(BF16), lower_bounds.astype(F32), norm_g.reshape(1, dv),
      w_out.astype(BF16), ln_g.reshape(1, d), ln_b.reshape(1, d),
      w3, jnp.asarray(um), jnp.asarray(mask))
    return out.reshape(b, t, d)


def _bias_tables(heads, dh):
    pk = np.zeros((4 * 128, heads * 128), np.float32)
    pq = np.zeros((heads * 128, 4 * 128), np.float32)
    for h in range(heads):
        for piece in range(3):
            pk[piece * 128 + h, 128 * h + dh + piece] = -1.0
            pk[3 * 128, 128 * h + dh + 3 + piece] = 1.0
            pq[128 * h + dh + 3 + piece, piece * 128 + h] = 1.0
            pq[128 * h + dh + piece, 3 * 128] = 1.0
    return pk, pq


def _kv_kernel(h_ref, wk_ref, wvt_ref, vones_ref, fgw_ref, fgb_ref, tril_ref, pk_ref,
               k_ref, vt_ref, c_ref, carry_ref):
    @pl.when(pl.program_id(1) == 0)
    def _():
        carry_ref[...] = jnp.zeros_like(carry_ref)

    hb = h_ref[...].astype(BF16)
    z = _dot(hb, fgw_ref[...]) + fgb_ref[...]
    lf = jnp.minimum(z, 0.0) - jnp.log(1.0 + jnp.exp(-jnp.abs(z)))
    hi, mid, lo = _split3(lf)
    tril = tril_ref[...]
    cs = _dot(tril, hi) + _dot(tril, mid) + _dot(tril, lo) + carry_ref[...]
    c_ref[...] = cs
    carry_ref[...] = cs[cs.shape[0] - 1:, :]
    chi, cmid, clo = _split3(cs * LOG2E)
    c4 = jnp.concatenate([chi, cmid, clo, jnp.ones_like(chi)], axis=1)
    k_ref[...] = (_dot(hb, wk_ref[...]) + _dot(c4, pk_ref[...])).astype(BF16)
    vt_ref[0, 0] = (_dot_nt(wvt_ref[...], hb) + vones_ref[...]).astype(BF16)


def _shared_kv(h3, kv_w, fg_w, fg_b, *, tb):
    b, t, d = h3.shape
    dim = kv_w.shape[1] // 2
    heads = fg_w.shape[1]
    dh = dim // heads
    assert heads <= 128 and dh + 6 <= 128 and dh < FOX_VROWS and t % tb == 0
    nt = t // tb
    fgw = jnp.zeros((d, 128), BF16).at[:, :heads].set(fg_w.astype(BF16))
    fgb = jnp.zeros((1, 128), F32).at[0, :heads].set(fg_b.astype(F32))
    tril = jnp.asarray(np.tril(np.ones((tb, tb), np.float32)), BF16)
    pk, _ = _bias_tables(heads, dh)
    wk = jnp.zeros((d, heads, 128), BF16).at[:, :, :dh].set(
        kv_w[:, :dim].astype(BF16).reshape(d, heads, dh)).reshape(d, heads * 128)
    wvt = jnp.zeros((heads, FOX_VROWS, d), BF16).at[:, :dh, :].set(
        kv_w[:, dim:].astype(BF16).T.reshape(heads, dh, d)).reshape(heads * FOX_VROWS, d)
    vones = np.zeros((heads, FOX_VROWS, tb), np.float32)
    vones[:, dh, :] = 1.0
    vones = jnp.asarray(vones.reshape(heads * FOX_VROWS, tb))
    k, vt, c = pl.pallas_call(
        _kv_kernel,
        grid=(b, nt),
        in_specs=[
            pl.BlockSpec((tb, d), lambda i, j: (i * nt + j, 0)),
            _resident((d, heads * 128)),
            _resident((heads * FOX_VROWS, d)),
            _resident((heads * FOX_VROWS, tb)),
            _resident((d, 128)),
            _resident((1, 128)),
            _resident((tb, tb)),
            _resident(pk.shape),
        ],
        out_specs=[
            pl.BlockSpec((tb, heads * 128), lambda i, j: (i * nt + j, 0)),
            pl.BlockSpec((1, 1, heads * FOX_VROWS, tb), lambda i, j: (i, j, 0, 0)),
            pl.BlockSpec((tb, 128), lambda i, j: (i * nt + j, 0)),
        ],
        out_shape=[
            jax.ShapeDtypeStruct((b * t, heads * 128), BF16),
            jax.ShapeDtypeStruct((b, nt, heads * FOX_VROWS, tb), BF16),
            jax.ShapeDtypeStruct((b * t, 128), F32),
        ],
        scratch_shapes=[pltpu.VMEM((1, 128), F32)],
        compiler_params=pltpu.CompilerParams(
            dimension_semantics=("parallel", "arbitrary"), vmem_limit_bytes=VMEM_LIMIT_BYTES),
        name="shared_kv",
    )(h3.reshape(b * t, d), wk, wvt, vones, fgw, fgb, tril, jnp.asarray(pk, BF16))
    return k.reshape(b, t, heads * 128), vt, c.reshape(b, t, 128)


def _fox_kernel(x_ref, wqt_ref, pq_ref, k_ref, vt_ref, ct_ref, wout_ref, lng_ref, lnb_ref,
                o_ref, qt_ref, m_ref, acc_ref, s_ref, att_ref, *, alpha, heads, dh, tq):
    i = pl.program_id(1)
    vr = FOX_VROWS
    xb = x_ref[...].astype(BF16)
    chi, cmid, clo = _split3(ct_ref[0] * LOG2E)
    c4 = jnp.concatenate([chi, cmid, clo, jnp.ones_like(chi)], axis=0)
    qt_ref[...] = (_dot_nt(wqt_ref[...], xb) * LOG2E + _dot(pq_ref[...], c4)).astype(BF16)
    m_ref[...] = jnp.full_like(m_ref, NEG_BIG)
    acc_ref[...] = jnp.zeros_like(acc_ref)

    def block(j, masked):
        rows = pl.ds(pl.multiple_of(j * tq, tq), tq)
        if masked:
            r = lax.broadcasted_iota(jnp.int32, (tq, tq), 0)
            cidx = lax.broadcasted_iota(jnp.int32, (tq, tq), 1)
            causal = r <= cidx

        def scores(h):
            s = _dot(k_ref[0, rows, 128 * h:128 * (h + 1)], qt_ref[128 * h:128 * (h + 1), :])
            if masked:
                s = jnp.where(causal, s, NEG_BIG)
            s_ref[h % FOX_AHEAD] = s

        def accumulate(h, corr, pv):
            hs = slice(vr * h, vr * (h + 1))
            acc_ref[hs, :] = corr * acc_ref[hs, :] + pv

        for h in range(FOX_AHEAD - 1):
            scores(h)
        pending = None
        for h in range(heads):
            if h + FOX_AHEAD - 1 < heads:
                scores(h + FOX_AHEAD - 1)
            s = s_ref[h % FOX_AHEAD]
            m_old = m_ref[h]
            m_new = jnp.maximum(m_old, jnp.max(s, axis=0, keepdims=True))
            p = jnp.exp2(s - m_new).astype(BF16)
            corr = jnp.exp2(m_old - m_new)
            m_ref[h] = m_new
            pv = _dot(vt_ref[0, j, vr * h:vr * (h + 1), :], p)
            if pending is not None:
                accumulate(*pending)
            pending = (h, corr, pv)
        accumulate(*pending)

    def body(j, carry):
        block(j, False)
        return carry

    lax.fori_loop(0, i, body, 0)
    block(i, True)

    for h in range(heads):
        num = acc_ref[vr * h:vr * h + dh, :]
        den = acc_ref[vr * h + dh:vr * h + dh + 1, :]
        att_ref[dh * h:dh * (h + 1), :] = (num / den).astype(BF16)
    y = alpha * x_ref[...] + _dot_tn(att_ref[...], wout_ref[...])
    o_ref[...] = _layer_norm(y, lng_ref[...], lnb_ref[...])


def _fox_layer(x3, k, vt, c, w_q, w_out, ln_g, ln_b, *, alpha, heads, tq):
    b, t, d = x3.shape
    dim = w_q.shape[1]
    dh = dim // heads
    assert t % tq == 0 and vt.shape == (b, t // tq, heads * FOX_VROWS, tq)
    nt = t // tq
    _, pq = _bias_tables(heads, dh)
    wqt = jnp.zeros((heads, 128, d), BF16).at[:, :dh, :].set(
        (w_q * dh ** -0.5).astype(BF16).T.reshape(heads, dh, d)).reshape(heads * 128, d)
    ct = c.transpose(0, 2, 1)
    kern = functools.partial(_fox_kernel, alpha=alpha, heads=heads, dh=dh, tq=tq)
    out = pl.pallas_call(
        kern,
        grid=(b, nt),
        in_specs=[
            pl.BlockSpec((tq, d), lambda i, j: (i * nt + j, 0)),
            _resident((heads * 128, d)),
            _resident(pq.shape),
            pl.BlockSpec((1, t, heads * 128), lambda i, j: (i, 0, 0)),
            pl.BlockSpec((1, nt, heads * FOX_VROWS, tq), lambda i, j: (i, 0, 0, 0)),
            pl.BlockSpec((1, 128, tq), lambda i, j: (i, 0, j)),
            _resident((dim, d)),
            _resident((1, d)),
            _resident((1, d)),
        ],
        out_specs=pl.BlockSpec((tq, d), lambda i, j: (i * nt + j, 0)),
        out_shape=jax.ShapeDtypeStruct((b * t, d), F32),
        scratch_shapes=[
            pltpu.VMEM((heads * 128, tq), BF16),
            pltpu.VMEM((heads, 1, tq), F32),
            pltpu.VMEM((heads * FOX_VROWS, tq), F32),
            pltpu.VMEM((FOX_AHEAD, tq, tq), F32),
            pltpu.VMEM((dim, tq), BF16),
        ],
        compiler_params=pltpu.CompilerParams(
            dimension_semantics=("parallel", "arbitrary"), vmem_limit_bytes=VMEM_LIMIT_BYTES),
        name="fox",
    )(x3.reshape(b * t, d), wqt, jnp.asarray(pq, BF16), k, vt, ct,
      w_out.astype(BF16), ln_g.reshape(1, d), ln_b.reshape(1, d))
    return out.reshape(b, t, d)


def kernel(x, ffn_ln_g, ffn_ln_b, ffn_w_gate_up, ffn_w_down, mix_ln_g, mix_ln_b, hg_w_in, hg_lower_bounds, hg_norm_g, hg_w_out, kv_w, kv_fg_w, kv_fg_b, fox_w_q, fox_w_out):
    b, t, d = x.shape
    depth = ffn_w_gate_up.shape[0]
    n_a = hg_w_in.shape[0]
    alpha = (2.0 * depth) ** 0.25
    fox_heads = kv_fg_w.shape[1]

    def ffn(x3, l, s):
        return _ffn(x3.reshape(b * t, d), ffn_w_gate_up[l, s], ffn_w_down[l, s],
                    ffn_ln_g[l, s], ffn_ln_b[l, s], alpha=alpha).reshape(b, t, d)

    shared = None
    for l in range(depth):
        x = ffn(x, l, 0)
        if l < n_a:
            x = _hgrn2_layer(x, hg_w_in[l], hg_lower_bounds, hg_norm_g[l], hg_w_out[l],
                             mix_ln_g[l], mix_ln_b[l], alpha=alpha, layer=l)
        else:
            k, vt, c = shared
            x = _fox_layer(x, k, vt, c, fox_w_q[l - n_a], fox_w_out[l - n_a],
                           mix_ln_g[l], mix_ln_b[l], alpha=alpha, heads=fox_heads, tq=FOX_BLOCK)
        x = ffn(x, l, 1)
        if l == n_a - 1:
            shared = _shared_kv(x, kv_w, kv_fg_w, kv_fg_b, tb=FOX_BLOCK)
    return x
```

```python
import functools

import numpy as np
import jax
import jax.numpy as jnp
from jax import lax
from jax.experimental import pallas as pl
from jax.experimental.pallas import tpu as pltpu

F32 = jnp.float32
BF16 = jnp.bfloat16

LN_EPS = 1e-5
RMS_EPS = 1e-6
HG_CHUNK = 64
SUBLANES = 8
VMEM_LIMIT_BYTES = 56 * 1024 * 1024
NEG_BIG = -1e30
FOX_BLOCK = 256
FOX_AHEAD = 8
FOX_VROWS = 80
LOG2E = 1.4426950408889634


def _dot(a, b):
    return jnp.dot(a, b, preferred_element_type=F32)


def _dot_nt(a, b):
    return lax.dot_general(a, b, (((1,), (1,)), ((), ())), preferred_element_type=F32)


def _dot_tn(a, b):
    return lax.dot_general(a, b, (((0,), (0,)), ((), ())), preferred_element_type=F32)


def _sigmoid(x):
    return 1.0 / (1.0 + jnp.exp(-x))


def _silu(x):
    return x * _sigmoid(x)


def _layer_norm(y, g, b):
    mu = jnp.mean(y, axis=-1, keepdims=True)
    d = y - mu
    var = jnp.mean(d * d, axis=-1, keepdims=True)
    return d * lax.rsqrt(var + LN_EPS) * g + b


def _resident(shape):
    nd = len(shape)
    return pl.BlockSpec(shape, lambda *_: (0,) * nd, pipeline_mode=pl.Buffered(1))


def _split3(x):
    hi = x.astype(BF16)
    r1 = x - hi.astype(F32)
    mid = r1.astype(BF16)
    lo = (r1 - mid.astype(F32)).astype(BF16)
    return hi, mid, lo


def _ffn_kernel(x_ref, wgu_ref, wd_ref, g_ref, b_ref, o_ref, xb_ref, acc_ref,
                *, alpha, n_chunks, fc):
    f = n_chunks * fc
    xb_ref[...] = x_ref[...].astype(BF16)

    def hidden(c):
        xb = xb_ref[...]
        gate = _dot(xb, wgu_ref[:, c * fc:(c + 1) * fc])
        up = _dot(xb, wgu_ref[:, f + c * fc:f + (c + 1) * fc])
        return (_silu(gate) * up).astype(BF16)

    h = hidden(0)
    for c in range(n_chunks):
        h_next = hidden(c + 1) if c + 1 < n_chunks else None
        d = _dot(h, wd_ref[c * fc:(c + 1) * fc, :])
        if c == 0:
            acc_ref[...] = d
        else:
            acc_ref[...] += d
        h = h_next
    y = alpha * x_ref[...] + 0.5 * acc_ref[...]
    o_ref[...] = _layer_norm(y, g_ref[...], b_ref[...])


def _ffn(x2, w_gate_up, w_down, g, b, *, alpha, tm=512, fc=256):
    n, d = x2.shape
    f = w_down.shape[0]
    nc = f // fc
    assert nc * fc == f and n % tm == 0
    kern = functools.partial(_ffn_kernel, alpha=alpha, n_chunks=nc, fc=fc)
    return pl.pallas_call(
        kern,
        grid=(n // tm,),
        in_specs=[
            pl.BlockSpec((tm, d), lambda i: (i, 0)),
            _resident((d, 2 * f)),
            _resident((f, d)),
            _resident((1, d)),
            _resident((1, d)),
        ],
        out_specs=pl.BlockSpec((tm, d), lambda i: (i, 0)),
        out_shape=jax.ShapeDtypeStruct((n, d), F32),
        scratch_shapes=[pltpu.VMEM((tm, d), BF16), pltpu.VMEM((tm, d), F32)],
        compiler_params=pltpu.CompilerParams(
            dimension_semantics=("parallel",), vmem_limit_bytes=VMEM_LIMIT_BYTES),
        name="ffn",
    )(x2, w_gate_up.astype(BF16), w_down.astype(BF16), g.reshape(1, d), b.reshape(1, d))


def _hgrn2_tables(c):
    levels = int(np.log2(c))
    coarse = sum(1 for l in range(levels) if (c >> l) // 2 >= SUBLANES)
    r = np.arange(c)
    w = np.zeros((1 + levels - coarse, c, c), np.float32)
    um = np.zeros((levels, c, 128), np.float32)
    mask = np.zeros((levels + 1, c, c), np.float32)
    w[0] = (r[None, :] <= r[:, None])
    for l in range(levels):
        m = c >> l
        p = (r // m) * m
        a = p + m // 2 - 1
        upper = r >= p + m // 2
        j = r[None, :]
        wu_ = (j > a[:, None]) & (j <= r[:, None])
        wl_ = (j > r[:, None]) & (j <= a[:, None])
        if l >= coarse:
            w[1 + l - coarse] = np.where(upper[:, None], wu_, wl_)
        um[l] = upper[:, None]
        mask[l] = (r[:, None] // m) == (r[None, :] // m)
    mask[levels] = np.eye(c)
    return w.reshape(-1, c), um, mask, levels, coarse


def _hgrn2_kernel(x_ref, win_ref, lbp_ref, ng_ref, wout_ref, lng_ref, lnb_ref,
                  w3_ref, um_ref, mask_ref,
                  o_ref,
                  q_ref, k_ref, lf_ref, v_ref, gt_ref, oc_ref, e_ref, st_ref,
                  *, alpha, layer, heads, dk, dv, levels, coarse, tb):
    c = HG_CHUNK
    hf = heads * dk
    hv = heads * dv

    @pl.when(pl.program_id(1) == 0)
    def _():
        st_ref[...] = jnp.zeros_like(st_ref)

    lbp = lbp_ref[...]
    ex = jnp.exp(lbp - jnp.max(lbp, axis=0, keepdims=True))
    lb = jnp.sum(ex[:layer + 1], axis=0, keepdims=True) / jnp.sum(ex, axis=0, keepdims=True)

    xb = x_ref[...].astype(BF16)
    q_ref[...] = _silu(_dot(xb, win_ref[:, 0:hf]))
    f = lb + (1.0 - lb) * _sigmoid(_dot(xb, win_ref[:, hf:2 * hf]))
    lf_ref[...] = jnp.log2(f)
    k_ref[...] = 1.0 - f
    v_ref[...] = _dot(xb, win_ref[:, 2 * hf:2 * hf + hv]).astype(BF16)
    gt_ref[...] = _silu(_dot(xb, win_ref[:, 2 * hf + hv:2 * hf + 2 * hv]))

    def exponents(ci):
        lf3 = jnp.concatenate(_split3(lf_ref[ci * c:(ci + 1) * c, :]), axis=0)
        e_ref[ci % 2] = _dot(w3_ref[...], lf3)

    def chunk(ci):
        rows = slice(ci * c, (ci + 1) * c)
        e = e_ref.at[ci % 2]

        def operands(h):
            ck = slice(h * dk, (h + 1) * dk)
            qh = q_ref[rows, ck]
            kh = k_ref[rows, ck]
            g = e[0:c, ck]
            pairs = []
            for l in range(coarse):
                m = c >> l
                half = m // 2
                qs, ks_ = [], []
                for p in range(0, c, m):
                    ga = e[p + half - 1:p + half, ck]
                    lo = slice(p, p + half)
                    up = slice(p + half, p + m)
                    zero = jnp.zeros((half, dk), F32)
                    qs += [zero, qh[up] * jnp.exp2(g[up] - ga)]
                    ks_ += [kh[lo] * jnp.exp2(ga - g[lo]), zero]
                pairs.append((jnp.concatenate(qs, axis=0).astype(BF16),
                              jnp.concatenate(ks_, axis=0).astype(BF16)))
            for l in range(coarse, levels):
                p = jnp.exp2(e[(1 + l - coarse) * c:(2 + l - coarse) * c, ck])
                pu = p * um_ref[l]
                pairs.append(((qh * pu).astype(BF16), (kh * (p - pu)).astype(BF16)))
            pairs.append((qh.astype(BF16), kh.astype(BF16)))
            qi = (qh * jnp.exp2(g)).astype(BF16)
            ks = (kh * jnp.exp2(e[c - 1:c, ck] - g)).astype(BF16)
            return pairs, qi, ks

        def level_scores(pairs):
            return [_dot_nt(ql, kl) for ql, kl in pairs]

        def finish(h, ss, qi, ks):
            ck = slice(h * dk, (h + 1) * dk)
            cv = slice(h * dv, (h + 1) * dv)
            a = ss[0]
            for l in range(1, levels + 1):
                a = a + ss[l] * mask_ref[l]
            vh = v_ref[rows, cv]
            st = st_ref[h]
            oc_ref[rows, cv] = _dot(a.astype(BF16), vh) + _dot_nt(qi, st.astype(BF16))
            st_ref[h] = st * jnp.exp2(e[c - 1:c, ck]) + _dot_tn(vh, ks)

        ops = {}
        scs = {}
        for step in range(heads + 2):
            if 0 <= step - 1 < heads:
                scs[step - 1] = level_scores(ops[step - 1][0])
            if step < heads:
                ops[step] = operands(step)
            if 0 <= step - 2 < heads:
                _, qi, ks = ops.pop(step - 2)
                finish(step - 2, scs.pop(step - 2), qi, ks)

    n_chunks = tb // c
    exponents(0)
    for ci in range(n_chunks):
        if ci + 1 < n_chunks:
            exponents(ci + 1)
        chunk(ci)

    ng = ng_ref[...]
    for h in range(heads):
        cv = slice(h * dv, (h + 1) * dv)
        oh = oc_ref[:, cv]
        ms = jnp.mean(oh * oh, axis=-1, keepdims=True)
        oc_ref[:, cv] = oh * lax.rsqrt(ms + RMS_EPS) * ng
    og = (oc_ref[...] * gt_ref[...]).astype(BF16)
    y = alpha * x_ref[...] + _dot(og, wout_ref[...])
    o_ref[...] = _layer_norm(y, lng_ref[...], lnb_ref[...])


def _hgrn2_layer(x3, w_in, lower_bounds, norm_g, w_out, ln_g, ln_b, *, alpha, layer, tb=256):
    b, t, d = x3.shape
    dv = norm_g.shape[-1]
    heads = w_out.shape[0] // dv
    hf = lower_bounds.shape[-1]
    dk = hf // heads
    hv = heads * dv
    assert dk == 128 and dv == 128 and t % tb == 0 and tb % HG_CHUNK == 0
    nt = t // tb
    w, um, mask, levels, coarse = _hgrn2_tables(HG_CHUNK)
    w3 = jnp.asarray(np.concatenate([w, w, w], axis=1), BF16)
    nr = lower_bounds.shape[0]
    kern = functools.partial(_hgrn2_kernel, alpha=alpha, layer=layer, heads=heads, dk=dk, dv=dv,
                             levels=levels, coarse=coarse, tb=tb)
    c = HG_CHUNK
    out = pl.pallas_call(
        kern,
        grid=(b, nt),
        in_specs=[
            pl.BlockSpec((tb, d), lambda i, j: (i * nt + j, 0)),
            _resident((d, 2 * hf + 2 * hv)),
            _resident((nr, hf)),
            _resident((1, dv)),
            _resident((hv, d)),
            _resident((1, d)),
            _resident((1, d)),
            _resident(w3.shape),
            _resident(um.shape),
            _resident(mask.shape),
        ],
        out_specs=pl.BlockSpec((tb, d), lambda i, j: (i * nt + j, 0)),
        out_shape=jax.ShapeDtypeStruct((b * t, d), F32),
        scratch_shapes=[
            pltpu.VMEM((tb, hf), F32),
            pltpu.VMEM((tb, hf), F32),
            pltpu.VMEM((tb, hf), F32),
            pltpu.VMEM((tb, hv), BF16),
            pltpu.VMEM((tb, hv), F32),
            pltpu.VMEM((tb, hv), F32),
            pltpu.VMEM((2, w.shape[0], hf), F32),
            pltpu.VMEM((heads, dv, dk), F32),
        ],
        compiler_params=pltpu.CompilerParams(
            dimension_semantics=("parallel", "arbitrary"), vmem_limit_bytes=VMEM_LIMIT_BYTES),
        name="hgrn2",
    )(x3.reshape(b * t, d), w_in.astype(BF16), lower_bounds.astype(F32), norm_g.reshape(1, dv),
      w_out.astype(BF16), ln_g.reshape(1, d), ln_b.reshape(1, d),
      w3, jnp.asarray(um), jnp.asarray(mask))
    return out.reshape(b, t, d)


def _bias_tables(heads, dh):
    pk = np.zeros((4 * 128, heads * 128), np.float32)
    pq = np.zeros((heads * 128, 4 * 128), np.float32)
    for h in range(heads):
        for piece in range(3):
            pk[piece * 128 + h, 128 * h + dh + piece] = -1.0
            pk[3 * 128, 128 * h + dh + 3 + piece] = 1.0
            pq[128 * h + dh + 3 + piece, piece * 128 + h] = 1.0
            pq[128 * h + dh + piece, 3 * 128] = 1.0
    return pk, pq


def _kv_kernel(h_ref, wk_ref, wvt_ref, vones_ref, fgw_ref, fgb_ref, tril_ref, pk_ref,
               k_ref, vt_ref, c_ref, carry_ref):
    @pl.when(pl.program_id(1) == 0)
    def _():
        carry_ref[...] = jnp.zeros_like(carry_ref)

    hb = h_ref[...].astype(BF16)
    z = _dot(hb, fgw_ref[...]) + fgb_ref[...]
    lf = jnp.minimum(z, 0.0) - jnp.log(1.0 + jnp.exp(-jnp.abs(z)))
    hi, mid, lo = _split3(lf)
    tril = tril_ref[...]
    cs = _dot(tril, hi) + _dot(tril, mid) + _dot(tril, lo) + carry_ref[...]
    c_ref[...] = cs
    carry_ref[...] = cs[cs.shape[0] - 1:, :]
    chi, cmid, clo = _split3(cs * LOG2E)
    c4 = jnp.concatenate([chi, cmid, clo, jnp.ones_like(chi)], axis=1)
    k_ref[...] = (_dot(hb, wk_ref[...]) + _dot(c4, pk_ref[...])).astype(BF16)
    vt_ref[0, 0] = (_dot_nt(wvt_ref[...], hb) + vones_ref[...]).astype(BF16)


def _shared_kv(h3, kv_w, fg_w, fg_b, *, tb):
    b, t, d = h3.shape
    dim = kv_w.shape[1] // 2
    heads = fg_w.shape[1]
    dh = dim // heads
    assert heads <= 128 and dh + 6 <= 128 and dh < FOX_VROWS and t % tb == 0
    nt = t // tb
    fgw = jnp.zeros((d, 128), BF16).at[:, :heads].set(fg_w.astype(BF16))
    fgb = jnp.zeros((1, 128), F32).at[0, :heads].set(fg_b.astype(F32))
    tril = jnp.asarray(np.tril(np.ones((tb, tb), np.float32)), BF16)
    pk, _ = _bias_tables(heads, dh)
    wk = jnp.zeros((d, heads, 128), BF16).at[:, :, :dh].set(
        kv_w[:, :dim].astype(BF16).reshape(d, heads, dh)).reshape(d, heads * 128)
    wvt = jnp.zeros((heads, FOX_VROWS, d), BF16).at[:, :dh, :].set(
        kv_w[:, dim:].astype(BF16).T.reshape(heads, dh, d)).reshape(heads * FOX_VROWS, d)
    vones = np.zeros((heads, FOX_VROWS, tb), np.float32)
    vones[:, dh, :] = 1.0
    vones = jnp.asarray(vones.reshape(heads * FOX_VROWS, tb))
    k, vt, c = pl.pallas_call(
        _kv_kernel,
        grid=(b, nt),
        in_specs=[
            pl.BlockSpec((tb, d), lambda i, j: (i * nt + j, 0)),
            _resident((d, heads * 128)),
            _resident((heads * FOX_VROWS, d)),
            _resident((heads * FOX_VROWS, tb)),
            _resident((d, 128)),
            _resident((1, 128)),
            _resident((tb, tb)),
            _resident(pk.shape),
        ],
        out_specs=[
            pl.BlockSpec((tb, heads * 128), lambda i, j: (i * nt + j, 0)),
            pl.BlockSpec((1, 1, heads * FOX_VROWS, tb), lambda i, j: (i, j, 0, 0)),
            pl.BlockSpec((tb, 128), lambda i, j: (i * nt + j, 0)),
        ],
        out_shape=[
            jax.ShapeDtypeStruct((b * t, heads * 128), BF16),
            jax.ShapeDtypeStruct((b, nt, heads * FOX_VROWS, tb), BF16),
            jax.ShapeDtypeStruct((b * t, 128), F32),
        ],
        scratch_shapes=[pltpu.VMEM((1, 128), F32)],
        compiler_params=pltpu.CompilerParams(
            dimension_semantics=("parallel", "arbitrary"), vmem_limit_bytes=VMEM_LIMIT_BYTES),
        name="shared_kv",
    )(h3.reshape(b * t, d), wk, wvt, vones, fgw, fgb, tril, jnp.asarray(pk, BF16))
    return k.reshape(b, t, heads * 128), vt, c.reshape(b, t, 128)


def _fox_kernel(x_ref, wqt_ref, pq_ref, k_ref, vt_ref, ct_ref, wout_ref, lng_ref, lnb_ref,
                o_ref, qt_ref, m_ref, acc_ref, s_ref, att_ref, *, alpha, heads, dh, tq):
    i = pl.program_id(1)
    vr = FOX_VROWS
    xb = x_ref[...].astype(BF16)
    chi, cmid, clo = _split3(ct_ref[0] * LOG2E)
    c4 = jnp.concatenate([chi, cmid, clo, jnp.ones_like(chi)], axis=0)
    qt_ref[...] = (_dot_nt(wqt_ref[...], xb) * LOG2E + _dot(pq_ref[...], c4)).astype(BF16)
    m_ref[...] = jnp.full_like(m_ref, NEG_BIG)
    acc_ref[...] = jnp.zeros_like(acc_ref)

    def block(j, masked):
        rows = pl.ds(pl.multiple_of(j * tq, tq), tq)
        if masked:
            r = lax.broadcasted_iota(jnp.int32, (tq, tq), 0)
            cidx = lax.broadcasted_iota(jnp.int32, (tq, tq), 1)
            causal = r <= cidx

        def scores(h):
            s = _dot(k_ref[0, rows, 128 * h:128 * (h + 1)], qt_ref[128 * h:128 * (h + 1), :])
            if masked:
                s = jnp.where(causal, s, NEG_BIG)
            s_ref[h % FOX_AHEAD] = s

        def accumulate(h, corr, pv):
            hs = slice(vr * h, vr * (h + 1))
            acc_ref[hs, :] = corr * acc_ref[hs, :] + pv

        for h in range(FOX_AHEAD - 1):
            scores(h)
        pending = None
        for h in range(heads):
            if h + FOX_AHEAD - 1 < heads:
                scores(h + FOX_AHEAD - 1)
            s = s_ref[h % FOX_AHEAD]
            m_old = m_ref[h]
            m_new = jnp.maximum(m_old, jnp.max(s, axis=0, keepdims=True))
            p = jnp.exp2(s - m_new).astype(BF16)
            corr = jnp.exp2(m_old - m_new)
            m_ref[h] = m_new
            pv = _dot(vt_ref[0, j, vr * h:vr * (h + 1), :], p)
            if pending is not None:
                accumulate(*pending)
            pending = (h, corr, pv)
        accumulate(*pending)

    def body(j, carry):
        block(j, False)
        return carry

    lax.fori_loop(0, i, body, 0)
    block(i, True)

    for h in range(heads):
        num = acc_ref[vr * h:vr * h + dh, :]
        den = acc_ref[vr * h + dh:vr * h + dh + 1, :]
        att_ref[dh * h:dh * (h + 1), :] = (num / den).astype(BF16)
    y = alpha * x_ref[...] + _dot_tn(att_ref[...], wout_ref[...])
    o_ref[...] = _layer_norm(y, lng_ref[...], lnb_ref[...])


def _fox_layer(x3, k, vt, c, w_q, w_out, ln_g, ln_b, *, alpha, heads, tq):
    b, t, d = x3.shape
    dim = w_q.shape[1]
    dh = dim // heads
    assert t % tq == 0 and vt.shape == (b, t // tq, heads * FOX_VROWS, tq)
    nt = t // tq
    _, pq = _bias_tables(heads, dh)
    wqt = jnp.zeros((heads, 128, d), BF16).at[:, :dh, :].set(
        (w_q * dh ** -0.5).astype(BF16).T.reshape(heads, dh, d)).reshape(heads * 128, d)
    ct = c.transpose(0, 2, 1)
    kern = functools.partial(_fox_kernel, alpha=alpha, heads=heads, dh=dh, tq=tq)
    out = pl.pallas_call(
        kern,
        grid=(b, nt),
        in_specs=[
            pl.BlockSpec((tq, d), lambda i, j: (i * nt + j, 0)),
            _resident((heads * 128, d)),
            _resident(pq.shape),
            pl.BlockSpec((1, t, heads * 128), lambda i, j: (i, 0, 0)),
            pl.BlockSpec((1, nt, heads * FOX_VROWS, tq), lambda i, j: (i, 0, 0, 0)),
            pl.BlockSpec((1, 128, tq), lambda i, j: (i, 0, j)),
            _resident((dim, d)),
            _resident((1, d)),
            _resident((1, d)),
        ],
        out_specs=pl.BlockSpec((tq, d), lambda i, j: (i * nt + j, 0)),
        out_shape=jax.ShapeDtypeStruct((b * t, d), F32),
        scratch_shapes=[
            pltpu.VMEM((heads * 128, tq), BF16),
            pltpu.VMEM((heads, 1, tq), F32),
            pltpu.VMEM((heads * FOX_VROWS, tq), F32),
            pltpu.VMEM((FOX_AHEAD, tq, tq), F32),
            pltpu.VMEM((dim, tq), BF16),
        ],
        compiler_params=pltpu.CompilerParams(
            dimension_semantics=("parallel", "arbitrary"), vmem_limit_bytes=VMEM_LIMIT_BYTES),
        name="fox",
    )(x3.reshape(b * t, d), wqt, jnp.asarray(pq, BF16), k, vt, ct,
      w_out.astype(BF16), ln_g.reshape(1, d), ln_b.reshape(1, d))
    return out.reshape(b, t, d)


def kernel(x, ffn_ln_g, ffn_ln_b, ffn_w_gate_up, ffn_w_down, mix_ln_g, mix_ln_b, hg_w_in, hg_lower_bounds, hg_norm_g, hg_w_out, kv_w, kv_fg_w, kv_fg_b, fox_w_q, fox_w_out):
    b, t, d = x.shape
    depth = ffn_w_gate_up.shape[0]
    n_a = hg_w_in.shape[0]
    alpha = (2.0 * depth) ** 0.25
    fox_heads = kv_fg_w.shape[1]

    def ffn(x3, l, s):
        return _ffn(x3.reshape(b * t, d), ffn_w_gate_up[l, s], ffn_w_down[l, s],
                    ffn_ln_g[l, s], ffn_ln_b[l, s], alpha=alpha).reshape(b, t, d)

    shared = None
    for l in range(depth):
        x = ffn(x, l, 0)
        if l < n_a:
            x = _hgrn2_layer(x, hg_w_in[l], hg_lower_bounds, hg_norm_g[l], hg_w_out[l],
                             mix_ln_g[l], mix_ln_b[l], alpha=alpha, layer=l)
        else:
            k, vt, c = shared
            x = _fox_layer(x, k, vt, c, fox_w_q[l - n_a], fox_w_out[l - n_a],
                           mix_ln_g[l], mix_ln_b[l], alpha=alpha, heads=fox_heads, tq=FOX_BLOCK)
        x = ffn(x, l, 1)
        if l == n_a - 1:
            shared = _shared_kv(x, kv_w, kv_fg_w, kv_fg_b, tb=FOX_BLOCK)
    return x
```

```python
import functools

import numpy as np
import jax
import jax.numpy as jnp
from jax import lax
from jax.experimental import pallas as pl
from jax.experimental.pallas import tpu as pltpu

F32 = jnp.float32
BF16 = jnp.bfloat16

LN_EPS = 1e-5
RMS_EPS = 1e-6
HG_CHUNK = 64
SUBLANES = 8
VMEM_LIMIT_BYTES = 56 * 1024 * 1024
NEG_BIG = -1e30
FOX_BLOCK = 256
FOX_AHEAD = 8
FOX_VROWS = 80
LOG2E = 1.4426950408889634
FOX_SKIP_LOG2 = 64.0
NORM_MARGIN = 1.02


def _dot(a, b):
    return jnp.dot(a, b, preferred_element_type=F32)


def _dot_nt(a, b):
    return lax.dot_general(a, b, (((1,), (1,)), ((), ())), preferred_element_type=F32)


def _dot_tn(a, b):
    return lax.dot_general(a, b, (((0,), (0,)), ((), ())), preferred_element_type=F32)


def _sigmoid(x):
    return 1.0 / (1.0 + jnp.exp(-x))


def _silu(x):
    return x * _sigmoid(x)


def _layer_norm(y, g, b):
    mu = jnp.mean(y, axis=-1, keepdims=True)
    d = y - mu
    var = jnp.mean(d * d, axis=-1, keepdims=True)
    return d * lax.rsqrt(var + LN_EPS) * g + b


def _resident(shape):
    nd = len(shape)
    return pl.BlockSpec(shape, lambda *_: (0,) * nd, pipeline_mode=pl.Buffered(1))


def _split3(x):
    hi = x.astype(BF16)
    r1 = x - hi.astype(F32)
    mid = r1.astype(BF16)
    lo = (r1 - mid.astype(F32)).astype(BF16)
    return hi, mid, lo


def _ffn_kernel(x_ref, xn_ref, wgu_ref, wd_ref, g_ref, b_ref, o_ref, xb_ref, acc_ref, acc0_ref,
                *, alpha, n_chunks, fc):
    f = n_chunks * fc

    def hidden(c):
        xb = xb_ref[...]
        gate = _dot(xb, wgu_ref[:, c * fc:(c + 1) * fc])
        up = _dot(xb, wgu_ref[:, f + c * fc:f + (c + 1) * fc])
        return (_silu(gate) * up).astype(BF16)

    def down(h, c):
        return _dot(h, wd_ref[c * fc:(c + 1) * fc, :])

    @pl.when(pl.program_id(0) == 0)
    def _():
        xb_ref[...] = x_ref[...].astype(BF16)
        acc0_ref[...] = down(hidden(0), 0)

    xb_ref[...] = x_ref[...].astype(BF16)
    h = hidden(1)
    for c in range(1, n_chunks):
        h_next = hidden(c + 1) if c + 1 < n_chunks else None
        d = down(h, c)
        if c == 1:
            acc_ref[...] = acc0_ref[...] + d
        else:
            acc_ref[...] += d
        h = h_next
    y = alpha * x_ref[...] + 0.5 * acc_ref[...]
    o_ref[...] = _layer_norm(y, g_ref[...], b_ref[...])
    xb_ref[...] = xn_ref[...].astype(BF16)
    acc0_ref[...] = down(hidden(0), 0)


def _ffn(x2, w_gate_up, w_down, g, b, *, alpha, tm=512, fc=256):
    n, d = x2.shape
    f = w_down.shape[0]
    nc = f // fc
    nt = n // tm
    assert nc * fc == f and nc >= 2 and nt * tm == n
    kern = functools.partial(_ffn_kernel, alpha=alpha, n_chunks=nc, fc=fc)
    return pl.pallas_call(
        kern,
        grid=(nt,),
        in_specs=[
            pl.BlockSpec((tm, d), lambda i: (i, 0)),
            pl.BlockSpec((tm, d), lambda i: (jnp.minimum(i + 1, nt - 1), 0)),
            _resident((d, 2 * f)),
            _resident((f, d)),
            _resident((1, d)),
            _resident((1, d)),
        ],
        out_specs=pl.BlockSpec((tm, d), lambda i: (i, 0)),
        out_shape=jax.ShapeDtypeStruct((n, d), F32),
        scratch_shapes=[pltpu.VMEM((tm, d), BF16), pltpu.VMEM((tm, d), F32),
                        pltpu.VMEM((tm, d), F32)],
        compiler_params=pltpu.CompilerParams(
            dimension_semantics=("arbitrary",), vmem_limit_bytes=VMEM_LIMIT_BYTES),
        name="ffn",
    )(x2, x2, w_gate_up.astype(BF16), w_down.astype(BF16), g.reshape(1, d), b.reshape(1, d))


def _hgrn2_tables(c):
    levels = int(np.log2(c))
    coarse = sum(1 for l in range(levels) if (c >> l) // 2 >= SUBLANES)
    r = np.arange(c)
    w = np.zeros((1 + levels - coarse, c, c), np.float32)
    um = np.zeros((levels, c, 128), np.float32)
    mask = np.zeros((levels + 1, c, c), np.float32)
    w[0] = (r[None, :] <= r[:, None])
    for l in range(levels):
        m = c >> l
        p = (r // m) * m
        a = p + m // 2 - 1
        upper = r >= p + m // 2
        j = r[None, :]
        wu_ = (j > a[:, None]) & (j <= r[:, None])
        wl_ = (j > r[:, None]) & (j <= a[:, None])
        if l >= coarse:
            w[1 + l - coarse] = np.where(upper[:, None], wu_, wl_)
        um[l] = upper[:, None]
        mask[l] = (r[:, None] // m) == (r[None, :] // m)
    mask[levels] = np.eye(c)
    return w.reshape(-1, c), um, mask, levels, coarse


def _hgrn2_kernel(x_ref, win_ref, lbp_ref, ng_ref, wout_ref, lng_ref, lnb_ref,
                  w3_ref, um_ref, mask_ref,
                  o_ref,
                  q_ref, k_ref, lf_ref, v_ref, gt_ref, oc_ref, e_ref, st_ref,
                  *, alpha, layer, heads, dk, dv, levels, coarse, tb):
    c = HG_CHUNK
    hf = heads * dk
    hv = heads * dv

    @pl.when(pl.program_id(1) == 0)
    def _():
        st_ref[...] = jnp.zeros_like(st_ref)

    lbp = lbp_ref[...]
    ex = jnp.exp(lbp - jnp.max(lbp, axis=0, keepdims=True))
    lb = jnp.sum(ex[:layer + 1], axis=0, keepdims=True) / jnp.sum(ex, axis=0, keepdims=True)

    xb = x_ref[...].astype(BF16)
    q_ref[...] = _silu(_dot(xb, win_ref[:, 0:hf]))
    f = lb + (1.0 - lb) * _sigmoid(_dot(xb, win_ref[:, hf:2 * hf]))
    lf_ref[...] = jnp.log2(f)
    k_ref[...] = 1.0 - f
    v_ref[...] = _dot(xb, win_ref[:, 2 * hf:2 * hf + hv]).astype(BF16)
    gt_ref[...] = _silu(_dot(xb, win_ref[:, 2 * hf + hv:2 * hf + 2 * hv]))

    def exponents(ci):
        lf3 = jnp.concatenate(_split3(lf_ref[ci * c:(ci + 1) * c, :]), axis=0)
        e_ref[ci % 2] = _dot(w3_ref[...], lf3)

    def chunk(ci):
        rows = slice(ci * c, (ci + 1) * c)
        e = e_ref.at[ci % 2]

        def operands(h):
            ck = slice(h * dk, (h + 1) * dk)
            qh = q_ref[rows, ck]
            kh = k_ref[rows, ck]
            g = e[0:c, ck]
            pairs = []
            for l in range(coarse):
                m = c >> l
                half = m // 2
                qs, ks_ = [], []
                for p in range(0, c, m):
                    ga = e[p + half - 1:p + half, ck]
                    lo = slice(p, p + half)
                    up = slice(p + half, p + m)
                    zero = jnp.zeros((half, dk), F32)
                    qs += [zero, qh[up] * jnp.exp2(g[up] - ga)]
                    ks_ += [kh[lo] * jnp.exp2(ga - g[lo]), zero]
                pairs.append((jnp.concatenate(qs, axis=0).astype(BF16),
                              jnp.concatenate(ks_, axis=0).astype(BF16)))
            for l in range(coarse, levels):
                p = jnp.exp2(e[(1 + l - coarse) * c:(2 + l - coarse) * c, ck])
                pu = p * um_ref[l]
                pairs.append(((qh * pu).astype(BF16), (kh * (p - pu)).astype(BF16)))
            pairs.append((qh.astype(BF16), kh.astype(BF16)))
            qi = (qh * jnp.exp2(g)).astype(BF16)
            ks = (kh * jnp.exp2(e[c - 1:c, ck] - g)).astype(BF16)
            return pairs, qi, ks

        def level_scores(pairs):
            return [_dot_nt(ql, kl) for ql, kl in pairs]

        def finish(h, ss, qi, ks):
            ck = slice(h * dk, (h + 1) * dk)
            cv = slice(h * dv, (h + 1) * dv)
            a = ss[0]
            for l in range(1, levels + 1):
                a = a + ss[l] * mask_ref[l]
            vh = v_ref[rows, cv]
            st = st_ref[h]
            oc_ref[rows, cv] = _dot(a.astype(BF16), vh) + _dot_nt(qi, st.astype(BF16))
            st_ref[h] = st * jnp.exp2(e[c - 1:c, ck]) + _dot_tn(vh, ks)

        ops = {}
        scs = {}
        for step in range(heads + 2):
            if 0 <= step - 1 < heads:
                scs[step - 1] = level_scores(ops[step - 1][0])
            if step < heads:
                ops[step] = operands(step)
            if 0 <= step - 2 < heads:
                _, qi, ks = ops.pop(step - 2)
                finish(step - 2, scs.pop(step - 2), qi, ks)

    n_chunks = tb // c
    exponents(0)
    for ci in range(n_chunks):
        if ci + 1 < n_chunks:
            exponents(ci + 1)
        chunk(ci)

    ng = ng_ref[...]
    for h in range(heads):
        cv = slice(h * dv, (h + 1) * dv)
        oh = oc_ref[:, cv]
        ms = jnp.mean(oh * oh, axis=-1, keepdims=True)
        oc_ref[:, cv] = oh * lax.rsqrt(ms + RMS_EPS) * ng
    og = (oc_ref[...] * gt_ref[...]).astype(BF16)
    y = alpha * x_ref[...] + _dot(og, wout_ref[...])
    o_ref[...] = _layer_norm(y, lng_ref[...], lnb_ref[...])


def _hgrn2_layer(x3, w_in, lower_bounds, norm_g, w_out, ln_g, ln_b, *, alpha, layer, tb=256):
    b, t, d = x3.shape
    dv = norm_g.shape[-1]
    heads = w_out.shape[0] // dv
    hf = lower_bounds.shape[-1]
    dk = hf // heads
    hv = heads * dv
    assert dk == 128 and dv == 128 and t % tb == 0 and tb % HG_CHUNK == 0
    nt = t // tb
    w, um, mask, levels, coarse = _hgrn2_tables(HG_CHUNK)
    w3 = jnp.asarray(np.concatenate([w, w, w], axis=1), BF16)
    nr = lower_bounds.shape[0]
    kern = functools.partial(_hgrn2_kernel, alpha=alpha, layer=layer, heads=heads, dk=dk, dv=dv,
                             levels=levels, coarse=coarse, tb=tb)
    c = HG_CHUNK
    out = pl.pallas_call(
        kern,
        grid=(b, nt),
        in_specs=[
            pl.BlockSpec((tb, d), lambda i, j: (i * nt + j, 0)),
            _resident((d, 2 * hf + 2 * hv)),
            _resident((nr, hf)),
            _resident((1, dv)),
            _resident((hv, d)),
            _resident((1, d)),
            _resident((1, d)),
            _resident(w3.shape),
            _resident(um.shape),
            _resident(mask.shape),
        ],
        out_specs=pl.BlockSpec((tb, d), lambda i, j: (i * nt + j, 0)),
        out_shape=jax.ShapeDtypeStruct((b * t, d), F32),
        scratch_shapes=[
            pltpu.VMEM((tb, hf), F32),
            pltpu.VMEM((tb, hf), F32),
            pltpu.VMEM((tb, hf), F32),
            pltpu.VMEM((tb, hv), BF16),
            pltpu.VMEM((tb, hv), F32),
            pltpu.VMEM((tb, hv), F32),
            pltpu.VMEM((2, w.shape[0], hf), F32),
            pltpu.VMEM((heads, dv, dk), F32),
        ],
        compiler_params=pltpu.CompilerParams(
            dimension_semantics=("parallel", "arbitrary"), vmem_limit_bytes=VMEM_LIMIT_BYTES),
        name="hgrn2",
    )(x3.reshape(b * t, d), w_in.astype(BF16), lower_bounds.astype(F32), norm_g.reshape(1, dv),
      w_out.astype(BF16), ln_g.reshape(1, d), ln_b.reshape(1, d),
      w3, jnp.asarray(um), jnp.asarray(mask))
    return out.reshape(b, t, d)


def _bias_tables(heads, dh):
    pk = np.zeros((4 * 128, heads * 128), np.float32)
    for h in range(heads):
        for piece in range(3):
            pk[piece * 128 + h, 128 * h + dh + piece] = -1.0
            pk[3 * 128, 128 * h + dh + 3 + piece] = 1.0
    return pk


def _kv_kernel(h_ref, wk_ref, wvt_ref, vones_ref, fgw_ref, fgb_ref, tril_ref, pk_ref, sel_ref,
               k_ref, vt_ref, c_ref, kn_ref, carry_ref, knmax_ref):
    @pl.when(pl.program_id(1) == 0)
    def _():
        carry_ref[...] = jnp.zeros_like(carry_ref)
        knmax_ref[...] = jnp.zeros_like(knmax_ref)

    hb = h_ref[...].astype(BF16)
    z = _dot(hb, fgw_ref[...]) + fgb_ref[...]
    lf = jnp.minimum(z, 0.0) - jnp.log(1.0 + jnp.exp(-jnp.abs(z)))
    hi, mid, lo = _split3(lf)
    tril = tril_ref[...]
    cs = _dot(tril, hi) + _dot(tril, mid) + _dot(tril, lo) + carry_ref[...]
    c_ref[...] = cs
    carry_ref[...] = cs[cs.shape[0] - 1:, :]
    chi, cmid, clo = _split3(cs * LOG2E)
    c4 = jnp.concatenate([chi, cmid, clo, jnp.ones_like(chi)], axis=1)
    kb = (_dot(hb, wk_ref[...]) + _dot(c4, pk_ref[...])).astype(BF16)
    k_ref[...] = kb
    vt_ref[0, 0] = (_dot_nt(wvt_ref[...], hb) + vones_ref[...]).astype(BF16)
    kf = kb.astype(F32)
    n2 = jnp.max(_dot((kf * kf).astype(BF16), sel_ref[...]), axis=0, keepdims=True)
    knmax_ref[...] = jnp.maximum(knmax_ref[...], n2)
    kn_ref[0] = jnp.broadcast_to(knmax_ref[...], kn_ref.shape[1:])


def _shared_kv(h3, kv_w, fg_w, fg_b, *, tb):
    b, t, d = h3.shape
    dim = kv_w.shape[1] // 2
    heads = fg_w.shape[1]
    dh = dim // heads
    assert heads <= 128 and dh + 6 <= 128 and dh < FOX_VROWS and t % tb == 0
    nt = t // tb
    fgw = jnp.zeros((d, 128), BF16).at[:, :heads].set(fg_w.astype(BF16))
    fgb = jnp.zeros((1, 128), F32).at[0, :heads].set(fg_b.astype(F32))
    tril = jnp.asarray(np.tril(np.ones((tb, tb), np.float32)), BF16)
    pk = _bias_tables(heads, dh)
    wk = jnp.zeros((d, heads, 128), BF16).at[:, :, :dh].set(
        kv_w[:, :dim].astype(BF16).reshape(d, heads, dh)).reshape(d, heads * 128)
    wvt = jnp.zeros((heads, FOX_VROWS, d), BF16).at[:, :dh, :].set(
        kv_w[:, dim:].astype(BF16).T.reshape(heads, dh, d)).reshape(heads * FOX_VROWS, d)
    vones = np.zeros((heads, FOX_VROWS, tb), np.float32)
    vones[:, dh, :] = 1.0
    vones = jnp.asarray(vones.reshape(heads * FOX_VROWS, tb))
    sel = np.zeros((heads, 128, 128), np.float32)
    for h in range(heads):
        sel[h, :dh, h] = 1.0
    sel = jnp.asarray(sel.reshape(heads * 128, 128), BF16)
    k, vt, c, kn = pl.pallas_call(
        _kv_kernel,
        grid=(b, nt),
        in_specs=[
            pl.BlockSpec((tb, d), lambda i, j: (i * nt + j, 0)),
            _resident((d, heads * 128)),
            _resident((heads * FOX_VROWS, d)),
            _resident((heads * FOX_VROWS, tb)),
            _resident((d, 128)),
            _resident((1, 128)),
            _resident((tb, tb)),
            _resident(pk.shape),
            _resident((heads * 128, 128)),
        ],
        out_specs=[
            pl.BlockSpec((tb, heads * 128), lambda i, j: (i * nt + j, 0)),
            pl.BlockSpec((1, 1, heads * FOX_VROWS, tb), lambda i, j: (i, j, 0, 0)),
            pl.BlockSpec((tb, 128), lambda i, j: (i * nt + j, 0)),
            pl.BlockSpec((1, SUBLANES, 128), lambda i, j: (i, 0, 0)),
        ],
        out_shape=[
            jax.ShapeDtypeStruct((b * t, heads * 128), BF16),
            jax.ShapeDtypeStruct((b, nt, heads * FOX_VROWS, tb), BF16),
            jax.ShapeDtypeStruct((b * t, 128), F32),
            jax.ShapeDtypeStruct((b, SUBLANES, 128), F32),
        ],
        scratch_shapes=[pltpu.VMEM((1, 128), F32), pltpu.VMEM((1, 128), F32)],
        compiler_params=pltpu.CompilerParams(
            dimension_semantics=("parallel", "arbitrary"), vmem_limit_bytes=VMEM_LIMIT_BYTES),
        name="shared_kv",
    )(h3.reshape(b * t, d), wk, wvt, vones, fgw, fgb, tril, jnp.asarray(pk, BF16), sel)
    return k.reshape(b, t, heads * 128), vt, c.reshape(b, t, 128), kn[:, 0, :]


def _fox_kernel(x_ref, wqt_ref, selq_ref, k_ref, vt_ref, ct_ref, kn_ref, cend_ref,
                wout_ref, lng_ref, lnb_ref,
                o_ref, qt_ref, m_ref, acc_ref, s_ref, att_ref, *, alpha, heads, dh, tq):
    i = pl.program_id(1)
    vr = FOX_VROWS
    xb = x_ref[...].astype(BF16)
    qt = _dot_nt(wqt_ref[...], xb) * LOG2E
    ctl = ct_ref[0] * LOG2E
    chi, cmid, clo = (p.astype(F32) for p in _split3(ctl))
    r16 = lax.broadcasted_iota(jnp.int32, (2 * SUBLANES, tq), 0)
    zpad = jnp.zeros((128 - dh - 2 * SUBLANES, tq), BF16)
    for h in range(heads):
        bias = jnp.where(r16 < 3, 1.0,
                         jnp.where(r16 == 3, chi[h:h + 1],
                                   jnp.where(r16 == 4, cmid[h:h + 1],
                                             jnp.where(r16 == 5, clo[h:h + 1], 0.0))))
        qt_ref[128 * h:128 * h + dh, :] = qt[dh * h:dh * (h + 1)].astype(BF16)
        qt_ref[128 * h + dh:128 * h + dh + 2 * SUBLANES, :] = bias.astype(BF16)
        qt_ref[128 * h + dh + 2 * SUBLANES:128 * (h + 1), :] = zpad
    m_ref[...] = jnp.full_like(m_ref, NEG_BIG)
    acc_ref[...] = jnp.zeros_like(acc_ref)

    qf = qt.astype(BF16).astype(F32)
    qn2 = jnp.max(_dot(selq_ref[...], (qf * qf).astype(BF16)), axis=1, keepdims=True)
    bound = 2.0 * jnp.sqrt(qn2 * NORM_MARGIN) * jnp.sqrt(kn_ref[0] * NORM_MARGIN)
    val = bound + ctl[:, 0:1] - cend_ref[0] * LOG2E
    hrow = lax.broadcasted_iota(jnp.int32, val.shape, 0)
    worst = jnp.max(jnp.where(hrow < heads, val, NEG_BIG), axis=0, keepdims=True)
    jcol = lax.broadcasted_iota(jnp.int32, worst.shape, 1)
    first = jnp.min(jnp.where((worst >= -FOX_SKIP_LOG2) & (jcol < i), jcol, i))

    def block(j, masked):
        rows = pl.ds(pl.multiple_of(j * tq, tq), tq)
        if masked:
            r = lax.broadcasted_iota(jnp.int32, (tq, tq), 0)
            cidx = lax.broadcasted_iota(jnp.int32, (tq, tq), 1)
            causal = r <= cidx

        def scores(h):
            s = _dot(k_ref[0, rows, 128 * h:128 * (h + 1)], qt_ref[128 * h:128 * (h + 1), :])
            if masked:
                s = jnp.where(causal, s, NEG_BIG)
            s_ref[h % FOX_AHEAD] = s

        def accumulate(h, corr, pv):
            hs = slice(vr * h, vr * (h + 1))
            acc_ref[hs, :] = corr * acc_ref[hs, :] + pv

        for h in range(FOX_AHEAD - 1):
            scores(h)
        pending = None
        for h in range(heads):
            if h + FOX_AHEAD - 1 < heads:
                scores(h + FOX_AHEAD - 1)
            s = s_ref[h % FOX_AHEAD]
            m_old = m_ref[h]
            m_new = jnp.maximum(m_old, jnp.max(s, axis=0, keepdims=True))
            p = jnp.exp2(s - m_new).astype(BF16)
            corr = jnp.exp2(m_old - m_new)
            m_ref[h] = m_new
            pv = _dot(vt_ref[0, j, vr * h:vr * (h + 1), :], p)
            if pending is not None:
                accumulate(*pending)
            pending = (h, corr, pv)
        accumulate(*pending)

    def body(j, carry):
        block(j, False)
        return carry

    lax.fori_loop(first, i, body, 0)
    block(i, True)

    for h in range(heads):
        num = acc_ref[vr * h:vr * h + dh, :]
        den = acc_ref[vr * h + dh:vr * h + dh + 1, :]
        att_ref[dh * h:dh * (h + 1), :] = (num / den).astype(BF16)
    y = alpha * x_ref[...] + _dot_tn(att_ref[...], wout_ref[...])
    o_ref[...] = _layer_norm(y, lng_ref[...], lnb_ref[...])


def _fox_layer(x3, k, vt, c, kn, w_q, w_out, ln_g, ln_b, *, alpha, heads, tq):
    b, t, d = x3.shape
    dim = w_q.shape[1]
    dh = dim // heads
    assert t % tq == 0 and vt.shape == (b, t // tq, heads * FOX_VROWS, tq)
    nt = t // tq
    assert heads <= 128 and nt <= 128 and dh + 2 * SUBLANES <= 128
    wqt = (w_q * dh ** -0.5).astype(BF16).T
    selq = np.zeros((128, heads, dh), np.float32)
    for h in range(heads):
        selq[h, h, :] = 1.0
    selq = jnp.asarray(selq.reshape(128, dim), BF16)
    ct = c.transpose(0, 2, 1)
    kn_rep = jnp.broadcast_to(kn[:, :, None], (b, 128, 128))
    cend = c.reshape(b, nt, tq, 128)[:, :, tq - 1, :].transpose(0, 2, 1)
    cend = jnp.pad(cend, ((0, 0), (0, 0), (0, 128 - nt)))
    kern = functools.partial(_fox_kernel, alpha=alpha, heads=heads, dh=dh, tq=tq)
    out = pl.pallas_call(
        kern,
        grid=(b, nt),
        in_specs=[
            pl.BlockSpec((tq, d), lambda i, j: (i * nt + j, 0)),
            _resident((dim, d)),
            _resident((128, dim)),
            pl.BlockSpec((1, t, heads * 128), lambda i, j: (i, 0, 0)),
            pl.BlockSpec((1, nt, heads * FOX_VROWS, tq), lambda i, j: (i, 0, 0, 0)),
            pl.BlockSpec((1, 128, tq), lambda i, j: (i, 0, j)),
            pl.BlockSpec((1, 128, 128), lambda i, j: (i, 0, 0)),
            pl.BlockSpec((1, 128, 128), lambda i, j: (i, 0, 0)),
            _resident((dim, d)),
            _resident((1, d)),
            _resident((1, d)),
        ],
        out_specs=pl.BlockSpec((tq, d), lambda i, j: (i * nt + j, 0)),
        out_shape=jax.ShapeDtypeStruct((b * t, d), F32),
        scratch_shapes=[
            pltpu.VMEM((heads * 128, tq), BF16),
            pltpu.VMEM((heads, 1, tq), F32),
            pltpu.VMEM((heads * FOX_VROWS, tq), F32),
            pltpu.VMEM((FOX_AHEAD, tq, tq), F32),
            pltpu.VMEM((dim, tq), BF16),
        ],
        compiler_params=pltpu.CompilerParams(
            dimension_semantics=("parallel", "arbitrary"), vmem_limit_bytes=VMEM_LIMIT_BYTES),
        name="fox",
    )(x3.reshape(b * t, d), wqt, selq, k, vt, ct, kn_rep, cend,
      w_out.astype(BF16), ln_g.reshape(1, d), ln_b.reshape(1, d))
    return out.reshape(b, t, d)


def kernel(x, ffn_ln_g, ffn_ln_b, ffn_w_gate_up, ffn_w_down, mix_ln_g, mix_ln_b, hg_w_in, hg_lower_bounds, hg_norm_g, hg_w_out, kv_w, kv_fg_w, kv_fg_b, fox_w_q, fox_w_out):
    b, t, d = x.shape
    depth = ffn_w_gate_up.shape[0]
    n_a = hg_w_in.shape[0]
    alpha = (2.0 * depth) ** 0.25
    fox_heads = kv_fg_w.shape[1]

    def ffn(x3, l, s):
        return _ffn(x3.reshape(b * t, d), ffn_w_gate_up[l, s], ffn_w_down[l, s],
                    ffn_ln_g[l, s], ffn_ln_b[l, s], alpha=alpha).reshape(b, t, d)

    shared = None
    for l in range(depth):
        x = ffn(x, l, 0)
        if l < n_a:
            x = _hgrn2_layer(x, hg_w_in[l], hg_lower_bounds, hg_norm_g[l], hg_w_out[l],
                             mix_ln_g[l], mix_ln_b[l], alpha=alpha, layer=l)
        else:
            k, vt, c, kn = shared
            x = _fox_layer(x, k, vt, c, kn, fox_w_q[l - n_a], fox_w_out[l - n_a],
                           mix_ln_g[l], mix_ln_b[l], alpha=alpha, heads=fox_heads, tq=FOX_BLOCK)
        x = ffn(x, l, 1)
        if l == n_a - 1:
            shared = _shared_kv(x, kv_w, kv_fg_w, kv_fg_b, tb=FOX_BLOCK)
    return x
```

```python
import functools

import numpy as np
import jax
import jax.numpy as jnp
from jax import lax
from jax.experimental import pallas as pl
from jax.experimental.pallas import tpu as pltpu

F32 = jnp.float32
BF16 = jnp.bfloat16

LN_EPS = 1e-5
RMS_EPS = 1e-6
HG_CHUNK = 64
SUBLANES = 8
VMEM_LIMIT_BYTES = 56 * 1024 * 1024
NEG_BIG = -1e30
FFN_TILE = 1024
FOX_BLOCK = 256
FOX_AHEAD = 8
FOX_VROWS = 80
LOG2E = 1.4426950408889634
FOX_SKIP_LOG2 = 64.0
NORM_MARGIN = 1.02


def _dot(a, b):
    return jnp.dot(a, b, preferred_element_type=F32)


def _dot_nt(a, b):
    return lax.dot_general(a, b, (((1,), (1,)), ((), ())), preferred_element_type=F32)


def _dot_tn(a, b):
    return lax.dot_general(a, b, (((0,), (0,)), ((), ())), preferred_element_type=F32)


def _sigmoid(x):
    return 1.0 / (1.0 + jnp.exp(-x))


def _silu(x):
    return x * _sigmoid(x)


def _layer_norm(y, g, b):
    mu = jnp.mean(y, axis=-1, keepdims=True)
    d = y - mu
    var = jnp.mean(d * d, axis=-1, keepdims=True)
    return d * lax.rsqrt(var + LN_EPS) * g + b


def _resident(shape):
    nd = len(shape)
    return pl.BlockSpec(shape, lambda *_: (0,) * nd, pipeline_mode=pl.Buffered(1))


def _split3(x):
    hi = x.astype(BF16)
    r1 = x - hi.astype(F32)
    mid = r1.astype(BF16)
    lo = (r1 - mid.astype(F32)).astype(BF16)
    return hi, mid, lo


def _ffn_kernel(x_ref, wgu_ref, wd_ref, g_ref, b_ref, o_ref, xb_ref, acc_ref,
                *, alpha, n_chunks, fc):
    f = n_chunks * fc
    xb_ref[...] = x_ref[...].astype(BF16)

    def hidden(c):
        xb = xb_ref[...]
        gate = _dot(xb, wgu_ref[:, c * fc:(c + 1) * fc])
        up = _dot(xb, wgu_ref[:, f + c * fc:f + (c + 1) * fc])
        return (_silu(gate) * up).astype(BF16)

    h = hidden(0)
    for c in range(n_chunks):
        h_next = hidden(c + 1) if c + 1 < n_chunks else None
        d = _dot(h, wd_ref[c * fc:(c + 1) * fc, :])
        if c == 0:
            acc_ref[...] = d
        else:
            acc_ref[...] += d
        h = h_next
    y = alpha * x_ref[...] + 0.5 * acc_ref[...]
    o_ref[...] = _layer_norm(y, g_ref[...], b_ref[...])


def _ffn(x2, w_gate_up_all, w_down_all, layer, half, g, b, *, alpha, tm=FFN_TILE, fc=256):
    n, d = x2.shape
    f = w_down_all.shape[2]
    nc = f // fc
    assert nc * fc == f and n % tm == 0

    def stacked(shape):
        return pl.BlockSpec((None, None) + shape, lambda i: (layer, half, 0, 0),
                            pipeline_mode=pl.Buffered(1))

    kern = functools.partial(_ffn_kernel, alpha=alpha, n_chunks=nc, fc=fc)
    return pl.pallas_call(
        kern,
        grid=(n // tm,),
        in_specs=[
            pl.BlockSpec((tm, d), lambda i: (i, 0)),
            stacked((d, 2 * f)),
            stacked((f, d)),
            _resident((1, d)),
            _resident((1, d)),
        ],
        out_specs=pl.BlockSpec((tm, d), lambda i: (i, 0)),
        out_shape=jax.ShapeDtypeStruct((n, d), F32),
        scratch_shapes=[pltpu.VMEM((tm, d), BF16), pltpu.VMEM((tm, d), F32)],
        compiler_params=pltpu.CompilerParams(
            dimension_semantics=("parallel",), vmem_limit_bytes=VMEM_LIMIT_BYTES),
        name="ffn",
    )(x2, w_gate_up_all, w_down_all, g.reshape(1, d), b.reshape(1, d))


def _hgrn2_tables(c):
    levels = int(np.log2(c)) - 1
    coarse = sum(1 for l in range(levels) if (c >> l) // 2 >= SUBLANES)
    r = np.arange(c)
    w = np.zeros((1 + levels - coarse, c, c), np.float32)
    um = np.zeros((levels, c, 128), np.float32)
    mask = np.zeros((levels, c, c), np.float32)
    w[0] = (r[None, :] <= r[:, None])
    for l in range(levels):
        m = c >> l
        p = (r // m) * m
        a = p + m // 2 - 1
        upper = r >= p + m // 2
        j = r[None, :]
        wu_ = (j > a[:, None]) & (j <= r[:, None])
        wl_ = (j > r[:, None]) & (j <= a[:, None])
        if l >= coarse:
            w[1 + l - coarse] = np.where(upper[:, None], wu_, wl_)
        um[l] = upper[:, None]
        mask[l] = (r[:, None] // m) == (r[None, :] // m)
    return w.reshape(-1, c), um, mask, levels, coarse


def _hgrn2_kernel(x_ref, win_ref, lbp_ref, ng_ref, wout_ref, lng_ref, lnb_ref,
                  w3_ref, um_ref, mask_ref,
                  o_ref,
                  q_ref, k_ref, lf_ref, v_ref, gt_ref, oc_ref, e_ref, st_ref,
                  *, alpha, layer, heads, dk, dv, levels, coarse, tb):
    c = HG_CHUNK
    hf = heads * dk
    hv = heads * dv

    @pl.when(pl.program_id(1) == 0)
    def _():
        st_ref[...] = jnp.zeros_like(st_ref)

    lbp = lbp_ref[...]
    ex = jnp.exp(lbp - jnp.max(lbp, axis=0, keepdims=True))
    lb = jnp.sum(ex[:layer + 1], axis=0, keepdims=True) / jnp.sum(ex, axis=0, keepdims=True)

    xb = x_ref[...].astype(BF16)
    q_ref[...] = _silu(_dot(xb, win_ref[:, 0:hf]))
    f = lb + (1.0 - lb) * _sigmoid(_dot(xb, win_ref[:, hf:2 * hf]))
    lf_ref[...] = jnp.log2(f)
    k_ref[...] = 1.0 - f
    v_ref[...] = _dot(xb, win_ref[:, 2 * hf:2 * hf + hv]).astype(BF16)
    gt_ref[...] = _silu(_dot(xb, win_ref[:, 2 * hf + hv:2 * hf + 2 * hv]))

    def exponents(ci):
        lf3 = jnp.concatenate(_split3(lf_ref[ci * c:(ci + 1) * c, :]), axis=0)
        e_ref[ci % 2] = _dot(w3_ref[...], lf3)

    def chunk(ci):
        rows = slice(ci * c, (ci + 1) * c)
        e = e_ref.at[ci % 2]

        def operands(h):
            ck = slice(h * dk, (h + 1) * dk)
            qh = q_ref[rows, ck]
            kh = k_ref[rows, ck]
            g = e[0:c, ck]
            pairs = []
            for l in range(coarse):
                m = c >> l
                half = m // 2
                qs, ks_ = [], []
                for p in range(0, c, m):
                    ga = e[p + half - 1:p + half, ck]
                    lo = slice(p, p + half)
                    up = slice(p + half, p + m)
                    zero = jnp.zeros((half, dk), F32)
                    qs += [zero, qh[up] * jnp.exp2(g[up] - ga)]
                    ks_ += [kh[lo] * jnp.exp2(ga - g[lo]), zero]
                pairs.append((jnp.concatenate(qs, axis=0).astype(BF16),
                              jnp.concatenate(ks_, axis=0).astype(BF16)))
            for l in range(coarse, levels):
                p = jnp.exp2(e[(1 + l - coarse) * c:(2 + l - coarse) * c, ck])
                pu = p * um_ref[l]
                pairs.append(((qh * pu).astype(BF16), (kh * (p - pu)).astype(BF16)))
            qi = (qh * jnp.exp2(g)).astype(BF16)
            ks = (kh * jnp.exp2(e[c - 1:c, ck] - g)).astype(BF16)
            return pairs, qi, ks

        def level_scores(pairs):
            return [_dot_nt(ql, kl) for ql, kl in pairs]

        def finish(h, ss, qi, ks):
            ck = slice(h * dk, (h + 1) * dk)
            cv = slice(h * dv, (h + 1) * dv)
            a = ss[0]
            for l in range(1, levels):
                a = a + ss[l] * mask_ref[l]
            vh = v_ref[rows, cv]
            qh = q_ref[rows, ck]
            kh = k_ref[rows, ck]
            vf = vh.astype(F32)
            d0 = jnp.sum(qh * kh, axis=-1, keepdims=True)
            d1 = jnp.sum(qh * (1.0 - kh) * pltpu.roll(kh, 1, 0), axis=-1, keepdims=True)
            odd = lax.broadcasted_iota(jnp.int32, d1.shape, 0) % 2 == 1
            near = d0 * vf + jnp.where(odd, d1, 0.0) * pltpu.roll(vf, 1, 0)
            st = st_ref[h]
            oc_ref[rows, cv] = _dot(a.astype(BF16), vh) + _dot_nt(qi, st.astype(BF16)) + near
            st_ref[h] = st * jnp.exp2(e[c - 1:c, ck]) + _dot_tn(vh, ks)

        ops = {}
        scs = {}
        for step in range(heads + 2):
            if 0 <= step - 1 < heads:
                scs[step - 1] = level_scores(ops[step - 1][0])
            if step < heads:
                ops[step] = operands(step)
            if 0 <= step - 2 < heads:
                _, qi, ks = ops.pop(step - 2)
                finish(step - 2, scs.pop(step - 2), qi, ks)

    n_chunks = tb // c
    exponents(0)
    for ci in range(n_chunks):
        if ci + 1 < n_chunks:
            exponents(ci + 1)
        chunk(ci)

    ng = ng_ref[...]
    for h in range(heads):
        cv = slice(h * dv, (h + 1) * dv)
        oh = oc_ref[:, cv]
        ms = jnp.mean(oh * oh, axis=-1, keepdims=True)
        oc_ref[:, cv] = oh * lax.rsqrt(ms + RMS_EPS) * ng
    og = (oc_ref[...] * gt_ref[...]).astype(BF16)
    y = alpha * x_ref[...] + _dot(og, wout_ref[...])
    o_ref[...] = _layer_norm(y, lng_ref[...], lnb_ref[...])


def _hgrn2_layer(x3, w_in, lower_bounds, norm_g, w_out, ln_g, ln_b, *, alpha, layer, tb=256):
    b, t, d = x3.shape
    dv = norm_g.shape[-1]
    heads = w_out.shape[0] // dv
    hf = lower_bounds.shape[-1]
    dk = hf // heads
    hv = heads * dv
    assert dk == 128 and dv == 128 and t % tb == 0 and tb % HG_CHUNK == 0
    nt = t // tb
    w, um, mask, levels, coarse = _hgrn2_tables(HG_CHUNK)
    w3 = jnp.asarray(np.concatenate([w, w, w], axis=1), BF16)
    nr = lower_bounds.shape[0]
    kern = functools.partial(_hgrn2_kernel, alpha=alpha, layer=layer, heads=heads, dk=dk, dv=dv,
                             levels=levels, coarse=coarse, tb=tb)
    c = HG_CHUNK
    out = pl.pallas_call(
        kern,
        grid=(b, nt),
        in_specs=[
            pl.BlockSpec((tb, d), lambda i, j: (i * nt + j, 0)),
            _resident((d, 2 * hf + 2 * hv)),
            _resident((nr, hf)),
            _resident((1, dv)),
            _resident((hv, d)),
            _resident((1, d)),
            _resident((1, d)),
            _resident(w3.shape),
            _resident(um.shape),
            _resident(mask.shape),
        ],
        out_specs=pl.BlockSpec((tb, d), lambda i, j: (i * nt + j, 0)),
        out_shape=jax.ShapeDtypeStruct((b * t, d), F32),
        scratch_shapes=[
            pltpu.VMEM((tb, hf), F32),
            pltpu.VMEM((tb, hf), F32),
            pltpu.VMEM((tb, hf), F32),
            pltpu.VMEM((tb, hv), BF16),
            pltpu.VMEM((tb, hv), F32),
            pltpu.VMEM((tb, hv), F32),
            pltpu.VMEM((2, w.shape[0], hf), F32),
            pltpu.VMEM((heads, dv, dk), F32),
        ],
        compiler_params=pltpu.CompilerParams(
            dimension_semantics=("parallel", "arbitrary"), vmem_limit_bytes=VMEM_LIMIT_BYTES),
        name="hgrn2",
    )(x3.reshape(b * t, d), w_in.astype(BF16), lower_bounds.astype(F32), norm_g.reshape(1, dv),
      w_out.astype(BF16), ln_g.reshape(1, d), ln_b.reshape(1, d),
      w3, jnp.asarray(um), jnp.asarray(mask))
    return out.reshape(b, t, d)


def _bias_tables(heads, dh):
    pk = np.zeros((128, heads * 128), np.float32)
    for h in range(heads):
        for piece in range(3):
            pk[piece * heads + h, 128 * h + dh + piece] = -1.0
            pk[3 * heads, 128 * h + dh + 3 + piece] = 1.0
    return pk


def _kv_kernel(h_ref, wk_ref, wvt_ref, vones_ref, fgw_ref, fgb_ref, tril_ref, pk_ref, sel_ref,
               k_ref, vt_ref, c_ref, kn_ref, carry_ref, knmax_ref, *, heads):
    @pl.when(pl.program_id(1) == 0)
    def _():
        carry_ref[...] = jnp.zeros_like(carry_ref)
        knmax_ref[...] = jnp.zeros_like(knmax_ref)

    hb = h_ref[...].astype(BF16)
    z = _dot(hb, fgw_ref[...]) + fgb_ref[...]
    lf = jnp.minimum(z, 0.0) - jnp.log(1.0 + jnp.exp(-jnp.abs(z)))
    hi, mid, lo = _split3(lf)
    tril = tril_ref[...]
    cs = _dot(tril, hi) + _dot(tril, mid) + _dot(tril, lo) + carry_ref[...]
    c_ref[...] = cs
    carry_ref[...] = cs[cs.shape[0] - 1:, :]
    chi, cmid, clo = _split3(cs * LOG2E)
    lane = lax.broadcasted_iota(jnp.int32, cs.shape, 1)
    c4 = jnp.where(lane < heads, chi, jnp.where(lane < 2 * heads, cmid, jnp.where(
        lane < 3 * heads, clo, jnp.where(lane == 3 * heads, 1.0, 0.0).astype(BF16))))
    kb = (_dot(hb, wk_ref[...]) + _dot(c4, pk_ref[...])).astype(BF16)
    k_ref[...] = kb
    vt_ref[0, 0] = (_dot_nt(wvt_ref[...], hb) + vones_ref[...]).astype(BF16)
    kf = kb.astype(F32)
    n2 = jnp.max(_dot((kf * kf).astype(BF16), sel_ref[...]), axis=0, keepdims=True)
    knmax_ref[...] = jnp.maximum(knmax_ref[...], n2)
    kn_ref[0] = jnp.broadcast_to(knmax_ref[...], kn_ref.shape[1:])


def _shared_kv(h3, kv_w, fg_w, fg_b, *, tb):
    b, t, d = h3.shape
    dim = kv_w.shape[1] // 2
    heads = fg_w.shape[1]
    dh = dim // heads
    assert 3 * heads < 128 and dh + 6 <= 128 and dh < FOX_VROWS and t % tb == 0
    nt = t // tb
    fgw = jnp.zeros((d, 128), BF16).at[:, :3 * heads].set(jnp.tile(fg_w.astype(BF16), (1, 3)))
    fgb = jnp.zeros((1, 128), F32).at[0, :3 * heads].set(jnp.tile(fg_b.astype(F32), 3))
    tril = jnp.asarray(np.tril(np.ones((tb, tb), np.float32)), BF16)
    pk = _bias_tables(heads, dh)
    wk = jnp.zeros((d, heads, 128), BF16).at[:, :, :dh].set(
        kv_w[:, :dim].astype(BF16).reshape(d, heads, dh)).reshape(d, heads * 128)
    wvt = jnp.zeros((heads, FOX_VROWS, d), BF16).at[:, :dh, :].set(
        kv_w[:, dim:].astype(BF16).T.reshape(heads, dh, d)).reshape(heads * FOX_VROWS, d)
    vones = np.zeros((heads, FOX_VROWS, tb), np.float32)
    vones[:, dh, :] = 1.0
    vones = jnp.asarray(vones.reshape(heads * FOX_VROWS, tb))
    sel = np.zeros((heads, 128, 128), np.float32)
    for h in range(heads):
        sel[h, :dh, h] = 1.0
    sel = jnp.asarray(sel.reshape(heads * 128, 128), BF16)
    k, vt, c, kn = pl.pallas_call(
        functools.partial(_kv_kernel, heads=heads),
        grid=(b, nt),
        in_specs=[
            pl.BlockSpec((tb, d), lambda i, j: (i * nt + j, 0)),
            _resident((d, heads * 128)),
            _resident((heads * FOX_VROWS, d)),
            _resident((heads * FOX_VROWS, tb)),
            _resident((d, 128)),
            _resident((1, 128)),
            _resident((tb, tb)),
            _resident(pk.shape),
            _resident((heads * 128, 128)),
        ],
        out_specs=[
            pl.BlockSpec((tb, heads * 128), lambda i, j: (i * nt + j, 0)),
            pl.BlockSpec((1, 1, heads * FOX_VROWS, tb), lambda i, j: (i, j, 0, 0)),
            pl.BlockSpec((tb, 128), lambda i, j: (i * nt + j, 0)),
            pl.BlockSpec((1, SUBLANES, 128), lambda i, j: (i, 0, 0)),
        ],
        out_shape=[
            jax.ShapeDtypeStruct((b * t, heads * 128), BF16),
            jax.ShapeDtypeStruct((b, nt, heads * FOX_VROWS, tb), BF16),
            jax.ShapeDtypeStruct((b * t, 128), F32),
            jax.ShapeDtypeStruct((b, SUBLANES, 128), F32),
        ],
        scratch_shapes=[pltpu.VMEM((1, 128), F32), pltpu.VMEM((1, 128), F32)],
        compiler_params=pltpu.CompilerParams(
            dimension_semantics=("parallel", "arbitrary"), vmem_limit_bytes=VMEM_LIMIT_BYTES),
        name="shared_kv",
    )(h3.reshape(b * t, d), wk, wvt, vones, fgw, fgb, tril, jnp.asarray(pk, BF16), sel)
    return k.reshape(b, t, heads * 128), vt, c.reshape(b, t, 128), kn[:, 0, :]


def _fox_kernel(x_ref, wqt_ref, selq_ref, k_ref, vt_ref, ct_ref, kn_ref, cend_ref,
                wout_ref, lng_ref, lnb_ref,
                o_ref, qt_ref, m_ref, acc_ref, s_ref, att_ref, *, alpha, heads, dh, tq):
    i = pl.program_id(1)
    vr = FOX_VROWS
    xb = x_ref[...].astype(BF16)
    qt = _dot_nt(wqt_ref[...], xb) * LOG2E
    ctl = ct_ref[0] * LOG2E
    chi, cmid, clo = (p.astype(F32) for p in _split3(ctl))
    r16 = lax.broadcasted_iota(jnp.int32, (2 * SUBLANES, tq), 0)
    zpad = jnp.zeros((128 - dh - 2 * SUBLANES, tq), BF16)
    for h in range(heads):
        bias = jnp.where(r16 < 3, 1.0,
                         jnp.where(r16 == 3, chi[h:h + 1],
                                   jnp.where(r16 == 4, cmid[h:h + 1],
                                             jnp.where(r16 == 5, clo[h:h + 1], 0.0))))
        qt_ref[128 * h:128 * h + dh, :] = qt[dh * h:dh * (h + 1)].astype(BF16)
        qt_ref[128 * h + dh:128 * h + dh + 2 * SUBLANES, :] = bias.astype(BF16)
        qt_ref[128 * h + dh + 2 * SUBLANES:128 * (h + 1), :] = zpad
    m_ref[...] = jnp.full_like(m_ref, NEG_BIG)
    acc_ref[...] = jnp.zeros_like(acc_ref)

    qf = qt.astype(BF16).astype(F32)
    qn2 = jnp.max(_dot(selq_ref[...], (qf * qf).astype(BF16)), axis=1, keepdims=True)
    bound = 2.0 * jnp.sqrt(qn2 * NORM_MARGIN) * jnp.sqrt(kn_ref[0] * NORM_MARGIN)
    val = bound + ctl[:, 0:1] - cend_ref[0] * LOG2E
    hrow = lax.broadcasted_iota(jnp.int32, val.shape, 0)
    worst = jnp.max(jnp.where(hrow < heads, val, NEG_BIG), axis=0, keepdims=True)
    jcol = lax.broadcasted_iota(jnp.int32, worst.shape, 1)
    first = jnp.min(jnp.where((worst >= -FOX_SKIP_LOG2) & (jcol < i), jcol, i))

    def block(j, masked):
        rows = pl.ds(pl.multiple_of(j * tq, tq), tq)
        if masked:
            r = lax.broadcasted_iota(jnp.int32, (tq, tq), 0)
            cidx = lax.broadcasted_iota(jnp.int32, (tq, tq), 1)
            causal = r <= cidx

        def scores(h):
            s = _dot(k_ref[0, rows, 128 * h:128 * (h + 1)], qt_ref[128 * h:128 * (h + 1), :])
            if masked:
                s = jnp.where(causal, s, NEG_BIG)
            s_ref[h % FOX_AHEAD] = s

        def accumulate(h, corr, pv):
            hs = slice(vr * h, vr * (h + 1))
            acc_ref[hs, :] = corr * acc_ref[hs, :] + pv

        for h in range(FOX_AHEAD - 1):
            scores(h)
        pending = None
        for h in range(heads):
            if h + FOX_AHEAD - 1 < heads:
                scores(h + FOX_AHEAD - 1)
            s = s_ref[h % FOX_AHEAD]
            m_old = m_ref[h]
            m_new = jnp.maximum(m_old, jnp.max(s, axis=0, keepdims=True))
            p = jnp.exp2(s - m_new).astype(BF16)
            corr = jnp.exp2(m_old - m_new)
            m_ref[h] = m_new
            pv = _dot(vt_ref[0, j, vr * h:vr * (h + 1), :], p)
            if pending is not None:
                accumulate(*pending)
            pending = (h, corr, pv)
        accumulate(*pending)

    def body(j, carry):
        block(j, False)
        return carry

    lax.fori_loop(first, i, body, 0)
    block(i, True)

    for h in range(heads):
        num = acc_ref[vr * h:vr * h + dh, :]
        den = acc_ref[vr * h + dh:vr * h + dh + 1, :]
        att_ref[dh * h:dh * (h + 1), :] = (num / den).astype(BF16)
    y = alpha * x_ref[...] + _dot_tn(att_ref[...], wout_ref[...])
    o_ref[...] = _layer_norm(y, lng_ref[...], lnb_ref[...])


def _fox_layer(x3, k, vt, c, kn, w_q, w_out, ln_g, ln_b, *, alpha, heads, tq):
    b, t, d = x3.shape
    dim = w_q.shape[1]
    dh = dim // heads
    assert t % tq == 0 and vt.shape == (b, t // tq, heads * FOX_VROWS, tq)
    nt = t // tq
    assert heads <= 128 and nt <= 128 and dh + 2 * SUBLANES <= 128
    wqt = (w_q * dh ** -0.5).astype(BF16).T
    selq = np.zeros((128, heads, dh), np.float32)
    for h in range(heads):
        selq[h, h, :] = 1.0
    selq = jnp.asarray(selq.reshape(128, dim), BF16)
    ct = c.transpose(0, 2, 1)
    kn_rep = jnp.broadcast_to(kn[:, :, None], (b, 128, 128))
    cend = c.reshape(b, nt, tq, 128)[:, :, tq - 1, :].transpose(0, 2, 1)
    cend = jnp.pad(cend, ((0, 0), (0, 0), (0, 128 - nt)))
    kern = functools.partial(_fox_kernel, alpha=alpha, heads=heads, dh=dh, tq=tq)
    out = pl.pallas_call(
        kern,
        grid=(b, nt),
        in_specs=[
            pl.BlockSpec((tq, d), lambda i, j: (i * nt + j, 0)),
            _resident((dim, d)),
            _resident((128, dim)),
            pl.BlockSpec((1, t, heads * 128), lambda i, j: (i, 0, 0)),
            pl.BlockSpec((1, nt, heads * FOX_VROWS, tq), lambda i, j: (i, 0, 0, 0)),
            pl.BlockSpec((1, 128, tq), lambda i, j: (i, 0, j)),
            pl.BlockSpec((1, 128, 128), lambda i, j: (i, 0, 0)),
            pl.BlockSpec((1, 128, 128), lambda i, j: (i, 0, 0)),
            _resident((dim, d)),
            _resident((1, d)),
            _resident((1, d)),
        ],
        out_specs=pl.BlockSpec((tq, d), lambda i, j: (i * nt + j, 0)),
        out_shape=jax.ShapeDtypeStruct((b * t, d), F32),
        scratch_shapes=[
            pltpu.VMEM((heads * 128, tq), BF16),
            pltpu.VMEM((heads, 1, tq), F32),
            pltpu.VMEM((heads * FOX_VROWS, tq), F32),
            pltpu.VMEM((FOX_AHEAD, tq, tq), F32),
            pltpu.VMEM((dim, tq), BF16),
        ],
        compiler_params=pltpu.CompilerParams(
            dimension_semantics=("parallel", "arbitrary"), vmem_limit_bytes=VMEM_LIMIT_BYTES),
        name="fox",
    )(x3.reshape(b * t, d), wqt, selq, k, vt, ct, kn_rep, cend,
      w_out.astype(BF16), ln_g.reshape(1, d), ln_b.reshape(1, d))
    return out.reshape(b, t, d)


def kernel(x, ffn_ln_g, ffn_ln_b, ffn_w_gate_up, ffn_w_down, mix_ln_g, mix_ln_b, hg_w_in, hg_lower_bounds, hg_norm_g, hg_w_out, kv_w, kv_fg_w, kv_fg_b, fox_w_q, fox_w_out):
    b, t, d = x.shape
    depth = ffn_w_gate_up.shape[0]
    n_a = hg_w_in.shape[0]
    alpha = (2.0 * depth) ** 0.25
    fox_heads = kv_fg_w.shape[1]

    wgu_all = ffn_w_gate_up.astype(BF16)
    wd_all = ffn_w_down.astype(BF16)

    def ffn(x3, l, s):
        return _ffn(x3.reshape(b * t, d), wgu_all, wd_all, l, s,
                    ffn_ln_g[l, s], ffn_ln_b[l, s], alpha=alpha).reshape(b, t, d)

    shared = None
    for l in range(depth):
        x = ffn(x, l, 0)
        if l < n_a:
            x = _hgrn2_layer(x, hg_w_in[l], hg_lower_bounds, hg_norm_g[l], hg_w_out[l],
                             mix_ln_g[l], mix_ln_b[l], alpha=alpha, layer=l)
        else:
            k, vt, c, kn = shared
            x = _fox_layer(x, k, vt, c, kn, fox_w_q[l - n_a], fox_w_out[l - n_a],
                           mix_ln_g[l], mix_ln_b[l], alpha=alpha, heads=fox_heads, tq=FOX_BLOCK)
        x = ffn(x, l, 1)
        if l == n_a - 1:
            shared = _shared_kv(x, kv_w, kv_fg_w, kv_fg_b, tb=FOX_BLOCK)
    return x
```

```python
import functools

import numpy as np
import jax
import jax.numpy as jnp
from jax import lax
from jax.experimental import pallas as pl
from jax.experimental.pallas import tpu as pltpu

F32 = jnp.float32
BF16 = jnp.bfloat16

LN_EPS = 1e-5
RMS_EPS = 1e-6
HG_CHUNK = 64
HG_TILE = 512
SUBLANES = 8
VMEM_LIMIT_BYTES = 56 * 1024 * 1024
NEG_BIG = -1e30
FFN_TILE = 1024
FOX_BLOCK = 256
KV_TILE = 512
FOX_AHEAD = 8
FOX_VROWS = 80
LOG2E = 1.4426950408889634
FOX_SKIP_LOG2 = 64.0
NORM_MARGIN = 1.02


def _dot(a, b):
    return jnp.dot(a, b, preferred_element_type=F32)


def _dot_nt(a, b):
    return lax.dot_general(a, b, (((1,), (1,)), ((), ())), preferred_element_type=F32)


def _dot_tn(a, b):
    return lax.dot_general(a, b, (((0,), (0,)), ((), ())), preferred_element_type=F32)


def _sigmoid(x):
    return 1.0 / (1.0 + jnp.exp(-x))


def _silu(x):
    return x * _sigmoid(x)


def _layer_norm(y, g, b):
    mu = jnp.mean(y, axis=-1, keepdims=True)
    d = y - mu
    var = jnp.mean(d * d, axis=-1, keepdims=True)
    return d * lax.rsqrt(var + LN_EPS) * g + b


def _resident(shape):
    nd = len(shape)
    return pl.BlockSpec(shape, lambda *_: (0,) * nd, pipeline_mode=pl.Buffered(1))


def _split3(x):
    hi = x.astype(BF16)
    r1 = x - hi.astype(F32)
    mid = r1.astype(BF16)
    lo = (r1 - mid.astype(F32)).astype(BF16)
    return hi, mid, lo


def _ffn_kernel(x_ref, wgu_ref, wd_ref, g_ref, b_ref, o_ref, xb_ref, acc_ref,
                *, alpha, n_chunks, fc):
    f = n_chunks * fc
    xb_ref[...] = x_ref[...].astype(BF16)

    def hidden(c):
        xb = xb_ref[...]
        gate = _dot(xb, wgu_ref[:, c * fc:(c + 1) * fc])
        up = _dot(xb, wgu_ref[:, f + c * fc:f + (c + 1) * fc])
        return (_silu(gate) * up).astype(BF16)

    h = hidden(0)
    for c in range(n_chunks):
        h_next = hidden(c + 1) if c + 1 < n_chunks else None
        d = _dot(h, wd_ref[c * fc:(c + 1) * fc, :])
        if c == 0:
            acc_ref[...] = d
        else:
            acc_ref[...] += d
        h = h_next
    y = alpha * x_ref[...] + 0.5 * acc_ref[...]
    o_ref[...] = _layer_norm(y, g_ref[...], b_ref[...])


def _ffn(x2, w_gate_up_all, w_down_all, layer, half, g, b, *, alpha, tm=FFN_TILE, fc=256):
    n, d = x2.shape
    f = w_down_all.shape[2]
    nc = f // fc
    assert nc * fc == f and n % tm == 0

    def stacked(shape):
        return pl.BlockSpec((None, None) + shape, lambda i: (layer, half, 0, 0),
                            pipeline_mode=pl.Buffered(1))

    kern = functools.partial(_ffn_kernel, alpha=alpha, n_chunks=nc, fc=fc)
    return pl.pallas_call(
        kern,
        grid=(n // tm,),
        in_specs=[
            pl.BlockSpec((tm, d), lambda i: (i, 0)),
            stacked((d, 2 * f)),
            stacked((f, d)),
            _resident((1, d)),
            _resident((1, d)),
        ],
        out_specs=pl.BlockSpec((tm, d), lambda i: (i, 0)),
        out_shape=jax.ShapeDtypeStruct((n, d), F32),
        scratch_shapes=[pltpu.VMEM((tm, d), BF16), pltpu.VMEM((tm, d), F32)],
        compiler_params=pltpu.CompilerParams(
            dimension_semantics=("parallel",), vmem_limit_bytes=VMEM_LIMIT_BYTES),
        name="ffn",
    )(x2, w_gate_up_all, w_down_all, g.reshape(1, d), b.reshape(1, d))


def _hgrn2_tables(c):
    levels = int(np.log2(c)) - 1
    coarse = sum(1 for l in range(levels) if (c >> l) // 2 >= SUBLANES)
    r = np.arange(c)
    w = np.zeros((1 + levels - coarse, c, c), np.float32)
    um = np.zeros((levels, c, 128), np.float32)
    mask = np.zeros((levels, c, c), np.float32)
    w[0] = (r[None, :] <= r[:, None])
    for l in range(levels):
        m = c >> l
        p = (r // m) * m
        a = p + m // 2 - 1
        upper = r >= p + m // 2
        j = r[None, :]
        wu_ = (j > a[:, None]) & (j <= r[:, None])
        wl_ = (j > r[:, None]) & (j <= a[:, None])
        if l >= coarse:
            w[1 + l - coarse] = np.where(upper[:, None], wu_, wl_)
        um[l] = upper[:, None]
        mask[l] = (r[:, None] // m) == (r[None, :] // m)
    return w.reshape(-1, c), um, mask, levels, coarse


def _hgrn2_kernel(x_ref, win_ref, lbp_ref, ng_ref, wout_ref, lng_ref, lnb_ref,
                  w3_ref, um_ref, mask_ref,
                  o_ref,
                  q_ref, k_ref, lf_ref, v_ref, gt_ref, oc_ref, e_ref, st_ref,
                  *, alpha, layer, heads, dk, dv, levels, coarse, tb):
    c = HG_CHUNK
    hf = heads * dk
    hv = heads * dv

    @pl.when(pl.program_id(1) == 0)
    def _():
        st_ref[...] = jnp.zeros_like(st_ref)

    lbp = lbp_ref[...]
    ex = jnp.exp(lbp - jnp.max(lbp, axis=0, keepdims=True))
    lb = jnp.sum(ex[:layer + 1], axis=0, keepdims=True) / jnp.sum(ex, axis=0, keepdims=True)

    xb = x_ref[...].astype(BF16)
    q_ref[...] = _silu(_dot(xb, win_ref[:, 0:hf]))
    f = lb + (1.0 - lb) * _sigmoid(_dot(xb, win_ref[:, hf:2 * hf]))
    lf_ref[...] = jnp.log2(f)
    k_ref[...] = 1.0 - f
    v_ref[...] = _dot(xb, win_ref[:, 2 * hf:2 * hf + hv]).astype(BF16)
    gt_ref[...] = _silu(_dot(xb, win_ref[:, 2 * hf + hv:2 * hf + 2 * hv]))

    def exponents(ci):
        lf3 = jnp.concatenate(_split3(lf_ref[ci * c:(ci + 1) * c, :]), axis=0)
        e_ref[ci] = _dot(w3_ref[...], lf3)

    def chunk(ci):
        rows = slice(ci * c, (ci + 1) * c)
        e = e_ref.at[ci]

        def operands(h):
            ck = slice(h * dk, (h + 1) * dk)
            qh = q_ref[rows, ck]
            kh = k_ref[rows, ck]
            g = e[0:c, ck]
            pairs = []
            for l in range(coarse):
                m = c >> l
                half = m // 2
                qs, ks_ = [], []
                for p in range(0, c, m):
                    ga = e[p + half - 1:p + half, ck]
                    lo = slice(p, p + half)
                    up = slice(p + half, p + m)
                    zero = jnp.zeros((half, dk), F32)
                    qs += [zero, qh[up] * jnp.exp2(g[up] - ga)]
                    ks_ += [kh[lo] * jnp.exp2(ga - g[lo]), zero]
                pairs.append((jnp.concatenate(qs, axis=0).astype(BF16),
                              jnp.concatenate(ks_, axis=0).astype(BF16)))
            for l in range(coarse, levels):
                p = jnp.exp2(e[(1 + l - coarse) * c:(2 + l - coarse) * c, ck])
                pu = p * um_ref[l]
                pairs.append(((qh * pu).astype(BF16), (kh * (p - pu)).astype(BF16)))
            qi = (qh * jnp.exp2(g)).astype(BF16)
            ks = (kh * jnp.exp2(e[c - 1:c, ck] - g)).astype(BF16)
            return pairs, qi, ks

        def level_scores(pairs):
            return [_dot_nt(ql, kl) for ql, kl in pairs]

        def finish(h, ss, qi, ks):
            ck = slice(h * dk, (h + 1) * dk)
            cv = slice(h * dv, (h + 1) * dv)
            a = ss[0]
            for l in range(1, levels):
                a = a + ss[l] * mask_ref[l]
            vh = v_ref[rows, cv]
            qh = q_ref[rows, ck]
            kh = k_ref[rows, ck]
            vf = vh.astype(F32)
            d0 = jnp.sum(qh * kh, axis=-1, keepdims=True)
            d1 = jnp.sum(qh * (1.0 - kh) * pltpu.roll(kh, 1, 0), axis=-1, keepdims=True)
            odd = lax.broadcasted_iota(jnp.int32, d1.shape, 0) % 2 == 1
            near = d0 * vf + jnp.where(odd, d1, 0.0) * pltpu.roll(vf, 1, 0)
            st = st_ref[h]
            oc_ref[rows, cv] = _dot(a.astype(BF16), vh) + _dot_nt(qi, st.astype(BF16)) + near
            st_ref[h] = st * jnp.exp2(e[c - 1:c, ck]) + _dot_tn(vh, ks)

        ops = {}
        scs = {}
        for step in range(heads + 2):
            if 0 <= step - 1 < heads:
                scs[step - 1] = level_scores(ops[step - 1][0])
            if step < heads:
                ops[step] = operands(step)
            if 0 <= step - 2 < heads:
                _, qi, ks = ops.pop(step - 2)
                finish(step - 2, scs.pop(step - 2), qi, ks)

    n_chunks = tb // c
    for ci in range(n_chunks):
        exponents(ci)
    for ci in range(n_chunks):
        chunk(ci)

    ng = ng_ref[...]
    for h in range(heads):
        cv = slice(h * dv, (h + 1) * dv)
        oh = oc_ref[:, cv]
        ms = jnp.mean(oh * oh, axis=-1, keepdims=True)
        oc_ref[:, cv] = oh * lax.rsqrt(ms + RMS_EPS) * ng
    og = (oc_ref[...] * gt_ref[...]).astype(BF16)
    y = alpha * x_ref[...] + _dot(og, wout_ref[...])
    o_ref[...] = _layer_norm(y, lng_ref[...], lnb_ref[...])


def _hgrn2_layer(x3, w_in, lower_bounds, norm_g, w_out, ln_g, ln_b, *, alpha, layer, tb=HG_TILE):
    b, t, d = x3.shape
    dv = norm_g.shape[-1]
    heads = w_out.shape[0] // dv
    hf = lower_bounds.shape[-1]
    dk = hf // heads
    hv = heads * dv
    assert dk == 128 and dv == 128 and t % tb == 0 and tb % HG_CHUNK == 0
    nt = t // tb
    w, um, mask, levels, coarse = _hgrn2_tables(HG_CHUNK)
    w3 = jnp.asarray(np.concatenate([w, w, w], axis=1), BF16)
    nr = lower_bounds.shape[0]
    kern = functools.partial(_hgrn2_kernel, alpha=alpha, layer=layer, heads=heads, dk=dk, dv=dv,
                             levels=levels, coarse=coarse, tb=tb)
    c = HG_CHUNK
    out = pl.pallas_call(
        kern,
        grid=(b, nt),
        in_specs=[
            pl.BlockSpec((tb, d), lambda i, j: (i * nt + j, 0)),
            _resident((d, 2 * hf + 2 * hv)),
            _resident((nr, hf)),
            _resident((1, dv)),
            _resident((hv, d)),
            _resident((1, d)),
            _resident((1, d)),
            _resident(w3.shape),
            _resident(um.shape),
            _resident(mask.shape),
        ],
        out_specs=pl.BlockSpec((tb, d), lambda i, j: (i * nt + j, 0)),
        out_shape=jax.ShapeDtypeStruct((b * t, d), F32),
        scratch_shapes=[
            pltpu.VMEM((tb, hf), F32),
            pltpu.VMEM((tb, hf), F32),
            pltpu.VMEM((tb, hf), F32),
            pltpu.VMEM((tb, hv), BF16),
            pltpu.VMEM((tb, hv), F32),
            pltpu.VMEM((tb, hv), F32),
            pltpu.VMEM((tb // c, w.shape[0], hf), F32),
            pltpu.VMEM((heads, dv, dk), F32),
        ],
        compiler_params=pltpu.CompilerParams(
            dimension_semantics=("parallel", "arbitrary"), vmem_limit_bytes=VMEM_LIMIT_BYTES),
        name="hgrn2",
    )(x3.reshape(b * t, d), w_in.astype(BF16), lower_bounds.astype(F32), norm_g.reshape(1, dv),
      w_out.astype(BF16), ln_g.reshape(1, d), ln_b.reshape(1, d),
      w3, jnp.asarray(um), jnp.asarray(mask))
    return out.reshape(b, t, d)


def _bias_tables(heads, dh):
    pk = np.zeros((128, heads * 128), np.float32)
    for h in range(heads):
        for piece in range(3):
            pk[piece * heads + h, 128 * h + dh + piece] = -1.0
            pk[3 * heads, 128 * h + dh + 3 + piece] = 1.0
    return pk


def _kv_kernel(h_ref, wk_ref, wvt_ref, vones_ref, fgw_ref, fgb_ref, tril_ref, pk_ref, sel_ref,
               k_ref, vt_ref, c_ref, kn_ref, carry_ref, knmax_ref, *, heads, dh, blk):
    @pl.when(pl.program_id(1) == 0)
    def _():
        carry_ref[...] = jnp.zeros_like(carry_ref)
        knmax_ref[...] = jnp.zeros_like(knmax_ref)

    hb = h_ref[...].astype(BF16)
    z = _dot(hb, fgw_ref[...]) + fgb_ref[...]
    lf = jnp.minimum(z, 0.0) - jnp.log(1.0 + jnp.exp(-jnp.abs(z)))
    hi, mid, lo = _split3(lf)
    tril = tril_ref[...]
    cs = _dot(tril, hi) + _dot(tril, mid) + _dot(tril, lo) + carry_ref[...]
    c_ref[...] = cs
    carry_ref[...] = cs[cs.shape[0] - 1:, :]
    chi, cmid, clo = _split3(cs * LOG2E)
    lane = lax.broadcasted_iota(jnp.int32, cs.shape, 1)
    c4 = jnp.where(lane < heads, chi, jnp.where(lane < 2 * heads, cmid, jnp.where(
        lane < 3 * heads, clo, jnp.where(lane == 3 * heads, 1.0, 0.0).astype(BF16))))
    bias = _dot(c4, pk_ref[...])
    ku = _dot(hb, wk_ref[...])
    low = lane < dh
    for p in range(heads // 2):
        pair = ku[:, 128 * p:128 * (p + 1)]
        even = slice(128 * 2 * p, 128 * (2 * p + 1))
        odd = slice(128 * (2 * p + 1), 128 * (2 * p + 2))
        k_ref[:, even] = jnp.where(low, pair, bias[:, even]).astype(BF16)
        k_ref[:, odd] = jnp.where(low, pltpu.roll(pair, 128 - dh, 1), bias[:, odd]).astype(BF16)
    vt = (_dot_nt(wvt_ref[...], hb) + vones_ref[...]).astype(BF16)
    for jb in range(vt.shape[1] // blk):
        vt_ref[0, jb] = vt[:, jb * blk:(jb + 1) * blk]
    kf = ku.astype(BF16).astype(F32)
    n2 = jnp.max(_dot((kf * kf).astype(BF16), sel_ref[...]), axis=0, keepdims=True)
    knmax_ref[...] = jnp.maximum(knmax_ref[...], n2)
    kn_ref[0] = jnp.broadcast_to(knmax_ref[...], kn_ref.shape[1:])


def _shared_kv(h3, kv_w, fg_w, fg_b, *, blk, tb=KV_TILE):
    b, t, d = h3.shape
    dim = kv_w.shape[1] // 2
    heads = fg_w.shape[1]
    dh = dim // heads
    assert 3 * heads < 128 and dh + 6 <= 128 and dh < FOX_VROWS and 2 * dh == 128 and heads % 2 == 0
    assert t % tb == 0 and tb % blk == 0
    nt = t // tb
    nsub = tb // blk
    fgw = jnp.zeros((d, 128), BF16).at[:, :3 * heads].set(jnp.tile(fg_w.astype(BF16), (1, 3)))
    fgb = jnp.zeros((1, 128), F32).at[0, :3 * heads].set(jnp.tile(fg_b.astype(F32), 3))
    tril = jnp.asarray(np.tril(np.ones((tb, tb), np.float32)), BF16)
    pk = _bias_tables(heads, dh)
    wk = kv_w[:, :dim].astype(BF16)
    wvt = jnp.zeros((heads, FOX_VROWS, d), BF16).at[:, :dh, :].set(
        kv_w[:, dim:].astype(BF16).T.reshape(heads, dh, d)).reshape(heads * FOX_VROWS, d)
    vones = np.zeros((heads, FOX_VROWS, tb), np.float32)
    vones[:, dh, :] = 1.0
    vones = jnp.asarray(vones.reshape(heads * FOX_VROWS, tb))
    sel = np.zeros((heads, dh, 128), np.float32)
    for h in range(heads):
        sel[h, :, h] = 1.0
    sel = jnp.asarray(sel.reshape(dim, 128), BF16)
    k, vt, c, kn = pl.pallas_call(
        functools.partial(_kv_kernel, heads=heads, dh=dh, blk=blk),
        grid=(b, nt),
        in_specs=[
            pl.BlockSpec((tb, d), lambda i, j: (i * nt + j, 0)),
            _resident((d, dim)),
            _resident((heads * FOX_VROWS, d)),
            _resident((heads * FOX_VROWS, tb)),
            _resident((d, 128)),
            _resident((1, 128)),
            _resident((tb, tb)),
            _resident(pk.shape),
            _resident((dim, 128)),
        ],
        out_specs=[
            pl.BlockSpec((tb, heads * 128), lambda i, j: (i * nt + j, 0)),
            pl.BlockSpec((1, nsub, heads * FOX_VROWS, blk), lambda i, j: (i, j, 0, 0)),
            pl.BlockSpec((tb, 128), lambda i, j: (i * nt + j, 0)),
            pl.BlockSpec((1, SUBLANES, 128), lambda i, j: (i, 0, 0)),
        ],
        out_shape=[
            jax.ShapeDtypeStruct((b * t, heads * 128), BF16),
            jax.ShapeDtypeStruct((b, t // blk, heads * FOX_VROWS, blk), BF16),
            jax.ShapeDtypeStruct((b * t, 128), F32),
            jax.ShapeDtypeStruct((b, SUBLANES, 128), F32),
        ],
        scratch_shapes=[pltpu.VMEM((1, 128), F32), pltpu.VMEM((1, 128), F32)],
        compiler_params=pltpu.CompilerParams(
            dimension_semantics=("parallel", "arbitrary"), vmem_limit_bytes=VMEM_LIMIT_BYTES),
        name="shared_kv",
    )(h3.reshape(b * t, d), wk, wvt, vones, fgw, fgb, tril, jnp.asarray(pk, BF16), sel)
    return k.reshape(b, t, heads * 128), vt, c.reshape(b, t, 128), kn[:, 0, :]


def _fox_kernel(x_ref, wqt_ref, selq_ref, k_ref, vt_ref, ct_ref, kn_ref, cend_ref,
                wout_ref, lng_ref, lnb_ref,
                o_ref, qt_ref, m_ref, acc_ref, s_ref, att_ref, *, alpha, heads, dh, tq):
    i = pl.program_id(1)
    vr = FOX_VROWS
    xb = x_ref[...].astype(BF16)
    qt = _dot_nt(wqt_ref[...], xb) * LOG2E
    ctl = ct_ref[0] * LOG2E
    chi, cmid, clo = (p.astype(F32) for p in _split3(ctl))
    r16 = lax.broadcasted_iota(jnp.int32, (2 * SUBLANES, tq), 0)
    zpad = jnp.zeros((128 - dh - 2 * SUBLANES, tq), BF16)
    for h in range(heads):
        bias = jnp.where(r16 < 3, 1.0,
                         jnp.where(r16 == 3, chi[h:h + 1],
                                   jnp.where(r16 == 4, cmid[h:h + 1],
                                             jnp.where(r16 == 5, clo[h:h + 1], 0.0))))
        qt_ref[128 * h:128 * h + dh, :] = qt[dh * h:dh * (h + 1)].astype(BF16)
        qt_ref[128 * h + dh:128 * h + dh + 2 * SUBLANES, :] = bias.astype(BF16)
        qt_ref[128 * h + dh + 2 * SUBLANES:128 * (h + 1), :] = zpad
    m_ref[...] = jnp.full_like(m_ref, NEG_BIG)
    acc_ref[...] = jnp.zeros_like(acc_ref)

    qf = qt.astype(BF16).astype(F32)
    qn2 = jnp.max(_dot(selq_ref[...], (qf * qf).astype(BF16)), axis=1, keepdims=True)
    bound = 2.0 * jnp.sqrt(qn2 * NORM_MARGIN) * jnp.sqrt(kn_ref[0] * NORM_MARGIN)
    val = bound + ctl[:, 0:1] - cend_ref[0] * LOG2E
    hrow = lax.broadcasted_iota(jnp.int32, val.shape, 0)
    worst = jnp.max(jnp.where(hrow < heads, val, NEG_BIG), axis=0, keepdims=True)
    jcol = lax.broadcasted_iota(jnp.int32, worst.shape, 1)
    first = jnp.min(jnp.where((worst >= -FOX_SKIP_LOG2) & (jcol < i), jcol, i))

    def block(j, masked):
        rows = pl.ds(pl.multiple_of(j * tq, tq), tq)
        if masked:
            r = lax.broadcasted_iota(jnp.int32, (tq, tq), 0)
            cidx = lax.broadcasted_iota(jnp.int32, (tq, tq), 1)
            causal = r <= cidx

        def scores(h):
            s = _dot(k_ref[0, rows, 128 * h:128 * (h + 1)], qt_ref[128 * h:128 * (h + 1), :])
            if masked:
                s = jnp.where(causal, s, NEG_BIG)
            s_ref[h % FOX_AHEAD] = s

        def accumulate(h, corr, pv):
            hs = slice(vr * h, vr * (h + 1))
            acc_ref[hs, :] = corr * acc_ref[hs, :] + pv

        for h in range(FOX_AHEAD - 1):
            scores(h)
        pending = None
        for h in range(heads):
            if h + FOX_AHEAD - 1 < heads:
                scores(h + FOX_AHEAD - 1)
            s = s_ref[h % FOX_AHEAD]
            m_old = m_ref[h]
            m_new = jnp.maximum(m_old, jnp.max(s, axis=0, keepdims=True))
            p = jnp.exp2(s - m_new).astype(BF16)
            corr = jnp.exp2(m_old - m_new)
            m_ref[h] = m_new
            pv = _dot(vt_ref[0, j, vr * h:vr * (h + 1), :], p)
            if pending is not None:
                accumulate(*pending)
            pending = (h, corr, pv)
        accumulate(*pending)

    def body(j, carry):
        block(j, False)
        return carry

    lax.fori_loop(first, i, body, 0)
    block(i, True)

    for h in range(heads):
        num = acc_ref[vr * h:vr * h + dh, :]
        den = acc_ref[vr * h + dh:vr * h + dh + 1, :]
        att_ref[dh * h:dh * (h + 1), :] = (num / den).astype(BF16)
    y = alpha * x_ref[...] + _dot_tn(att_ref[...], wout_ref[...])
    o_ref[...] = _layer_norm(y, lng_ref[...], lnb_ref[...])


def _fox_layer(x3, k, vt, c, kn, w_q, w_out, ln_g, ln_b, *, alpha, heads, tq):
    b, t, d = x3.shape
    dim = w_q.shape[1]
    dh = dim // heads
    assert t % tq == 0 and vt.shape == (b, t // tq, heads * FOX_VROWS, tq)
    nt = t // tq
    assert heads <= 128 and nt <= 128 and dh + 2 * SUBLANES <= 128
    wqt = (w_q * dh ** -0.5).astype(BF16).T
    selq = np.zeros((128, heads, dh), np.float32)
    for h in range(heads):
        selq[h, h, :] = 1.0
    selq = jnp.asarray(selq.reshape(128, dim), BF16)
    ct = c.transpose(0, 2, 1)
    kn_rep = jnp.broadcast_to(kn[:, :, None], (b, 128, 128))
    cend = c.reshape(b, nt, tq, 128)[:, :, tq - 1, :].transpose(0, 2, 1)
    cend = jnp.pad(cend, ((0, 0), (0, 0), (0, 128 - nt)))
    kern = functools.partial(_fox_kernel, alpha=alpha, heads=heads, dh=dh, tq=tq)
    out = pl.pallas_call(
        kern,
        grid=(b, nt),
        in_specs=[
            pl.BlockSpec((tq, d), lambda i, j: (i * nt + j, 0)),
            _resident((dim, d)),
            _resident((128, dim)),
            pl.BlockSpec((1, t, heads * 128), lambda i, j: (i, 0, 0)),
            pl.BlockSpec((1, nt, heads * FOX_VROWS, tq), lambda i, j: (i, 0, 0, 0)),
            pl.BlockSpec((1, 128, tq), lambda i, j: (i, 0, j)),
            pl.BlockSpec((1, 128, 128), lambda i, j: (i, 0, 0)),
            pl.BlockSpec((1, 128, 128), lambda i, j: (i, 0, 0)),
            _resident((dim, d)),
            _resident((1, d)),
            _resident((1, d)),
        ],
        out_specs=pl.BlockSpec((tq, d), lambda i, j: (i * nt + j, 0)),
        out_shape=jax.ShapeDtypeStruct((b * t, d), F32),
        scratch_shapes=[
            pltpu.VMEM((heads * 128, tq), BF16),
            pltpu.VMEM((heads, 1, tq), F32),
            pltpu.VMEM((heads * FOX_VROWS, tq), F32),
            pltpu.VMEM((FOX_AHEAD, tq, tq), F32),
            pltpu.VMEM((dim, tq), BF16),
        ],
        compiler_params=pltpu.CompilerParams(
            dimension_semantics=("parallel", "arbitrary"), vmem_limit_bytes=VMEM_LIMIT_BYTES),
        name="fox",
    )(x3.reshape(b * t, d), wqt, selq, k, vt, ct, kn_rep, cend,
      w_out.astype(BF16), ln_g.reshape(1, d), ln_b.reshape(1, d))
    return out.reshape(b, t, d)


def kernel(x, ffn_ln_g, ffn_ln_b, ffn_w_gate_up, ffn_w_down, mix_ln_g, mix_ln_b, hg_w_in, hg_lower_bounds, hg_norm_g, hg_w_out, kv_w, kv_fg_w, kv_fg_b, fox_w_q, fox_w_out):
    b, t, d = x.shape
    depth = ffn_w_gate_up.shape[0]
    n_a = hg_w_in.shape[0]
    alpha = (2.0 * depth) ** 0.25
    fox_heads = kv_fg_w.shape[1]

    wgu_all = ffn_w_gate_up.astype(BF16)
    wd_all = ffn_w_down.astype(BF16)

    def ffn(x3, l, s):
        return _ffn(x3.reshape(b * t, d), wgu_all, wd_all, l, s,
                    ffn_ln_g[l, s], ffn_ln_b[l, s], alpha=alpha).reshape(b, t, d)

    shared = None
    for l in range(depth):
        x = ffn(x, l, 0)
        if l < n_a:
            x = _hgrn2_layer(x, hg_w_in[l], hg_lower_bounds, hg_norm_g[l], hg_w_out[l],
                             mix_ln_g[l], mix_ln_b[l], alpha=alpha, layer=l)
        else:
            k, vt, c, kn = shared
            x = _fox_layer(x, k, vt, c, kn, fox_w_q[l - n_a], fox_w_out[l - n_a],
                           mix_ln_g[l], mix_ln_b[l], alpha=alpha, heads=fox_heads, tq=FOX_BLOCK)
        x = ffn(x, l, 1)
        if l == n_a - 1:
            shared = _shared_kv(x, kv_w, kv_fg_w, kv_fg_b, blk=FOX_BLOCK)
    return x
```

```python
import functools

import numpy as np
import jax
import jax.numpy as jnp
from jax import lax
from jax.experimental import pallas as pl
from jax.experimental.pallas import tpu as pltpu

F32 = jnp.float32
BF16 = jnp.bfloat16

LN_EPS = 1e-5
RMS_EPS = 1e-6
HG_CHUNK = 64
HG_TILE = 512
HG_SLOTS = 6
SUBLANES = 8
VMEM_LIMIT_BYTES = 56 * 1024 * 1024
NEG_BIG = -1e30
FFN_TILE = 1024
FFN_CHUNK = 256
FOX_BLOCK = 256
KV_TILE = 512
FOX_AHEAD = 8
FOX_VROWS = 80
LOG2E = 1.4426950408889634
FOX_SKIP_LOG2 = 64.0
NORM_MARGIN = 1.02


def _dot(a, b):
    return jnp.dot(a, b, preferred_element_type=F32)


def _dot_nt(a, b):
    return lax.dot_general(a, b, (((1,), (1,)), ((), ())), preferred_element_type=F32)


def _dot_tn(a, b):
    return lax.dot_general(a, b, (((0,), (0,)), ((), ())), preferred_element_type=F32)


def _sigmoid(x):
    return 1.0 / (1.0 + jnp.exp(-x))


def _silu(x):
    return x * _sigmoid(x)


def _layer_norm(y, g, b):
    mu = jnp.mean(y, axis=-1, keepdims=True)
    d = y - mu
    var = jnp.mean(d * d, axis=-1, keepdims=True)
    return d * lax.rsqrt(var + LN_EPS) * g + b


def _resident(shape):
    nd = len(shape)
    return pl.BlockSpec(shape, lambda *_: (0,) * nd, pipeline_mode=pl.Buffered(1))


def _split3(x):
    hi = x.astype(BF16)
    r1 = x - hi.astype(F32)
    mid = r1.astype(BF16)
    lo = (r1 - mid.astype(F32)).astype(BF16)
    return hi, mid, lo


def _ffn_kernel(x_ref, wgu_ref, wd_ref, g_ref, b_ref, o_ref, xb_ref, acc_ref,
                *, alpha, f, fc):
    chunks = [(lo, min(lo + fc, f)) for lo in range(0, f, fc)]
    xb_ref[...] = x_ref[...].astype(BF16)

    def hidden(c):
        lo, hi = chunks[c]
        xb = xb_ref[...]
        gate = _dot(xb, wgu_ref[:, lo:hi])
        up = _dot(xb, wgu_ref[:, f + lo:f + hi])
        return (_silu(gate) * up).astype(BF16)

    n_chunks = len(chunks)
    h = hidden(0)
    for c in range(n_chunks):
        h_next = hidden(c + 1) if c + 1 < n_chunks else None
        d = _dot(h, wd_ref[chunks[c][0]:chunks[c][1], :])
        if c == 0:
            acc_ref[...] = d
        else:
            acc_ref[...] += d
        h = h_next
    y = alpha * x_ref[...] + 0.5 * acc_ref[...]
    o_ref[...] = _layer_norm(y, g_ref[...], b_ref[...])


def _ffn(x2, w_gate_up_all, w_down_all, layer, half, g, b, *, alpha, tm=FFN_TILE, fc=FFN_CHUNK):
    n, d = x2.shape
    f = w_down_all.shape[2]
    assert f % 256 == 0 and fc % 256 == 0 and n % tm == 0

    def stacked(shape):
        return pl.BlockSpec((None, None) + shape, lambda i: (layer, half, 0, 0),
                            pipeline_mode=pl.Buffered(1))

    kern = functools.partial(_ffn_kernel, alpha=alpha, f=f, fc=fc)
    return pl.pallas_call(
        kern,
        grid=(n // tm,),
        in_specs=[
            pl.BlockSpec((tm, d), lambda i: (i, 0)),
            stacked((d, 2 * f)),
            stacked((f, d)),
            _resident((1, d)),
            _resident((1, d)),
        ],
        out_specs=pl.BlockSpec((tm, d), lambda i: (i, 0)),
        out_shape=jax.ShapeDtypeStruct((n, d), F32),
        scratch_shapes=[pltpu.VMEM((tm, d), BF16), pltpu.VMEM((tm, d), F32)],
        compiler_params=pltpu.CompilerParams(
            dimension_semantics=("parallel",), vmem_limit_bytes=VMEM_LIMIT_BYTES),
        name="ffn",
    )(x2, w_gate_up_all, w_down_all, g.reshape(1, d), b.reshape(1, d))


def _hgrn2_tables(c):
    levels = int(np.log2(c)) - 1
    coarse = sum(1 for l in range(levels) if (c >> l) // 2 >= SUBLANES)
    r = np.arange(c)
    w = np.zeros((1 + levels - coarse, c, c), np.float32)
    um = np.zeros((levels, c, 128), np.float32)
    mask = np.zeros((levels, c, c), np.float32)
    w[0] = (r[None, :] <= r[:, None])
    for l in range(levels):
        m = c >> l
        p = (r // m) * m
        a = p + m // 2 - 1
        upper = r >= p + m // 2
        j = r[None, :]
        wu_ = (j > a[:, None]) & (j <= r[:, None])
        wl_ = (j > r[:, None]) & (j <= a[:, None])
        if l >= coarse:
            w[1 + l - coarse] = np.where(upper[:, None], wu_, wl_)
        um[l] = upper[:, None]
        mask[l] = (r[:, None] // m) == (r[None, :] // m)
    return w.reshape(-1, c), um, mask, levels, coarse


def _hgrn2_kernel(x_ref, win_ref, lbp_ref, ng_ref, wout_ref, lng_ref, lnb_ref,
                  w3_ref, um_ref, mask_ref,
                  o_ref,
                  q_ref, k_ref, lf_ref, v_ref, gt_ref, oc_ref, e_ref, st_ref, sc_ref,
                  *, alpha, layer, heads, dk, dv, levels, coarse, tb):
    c = HG_CHUNK
    hf = heads * dk
    hv = heads * dv

    @pl.when(pl.program_id(1) == 0)
    def _():
        st_ref[...] = jnp.zeros_like(st_ref)

    lbp = lbp_ref[...]
    ex = jnp.exp(lbp - jnp.max(lbp, axis=0, keepdims=True))
    lb = jnp.sum(ex[:layer + 1], axis=0, keepdims=True) / jnp.sum(ex, axis=0, keepdims=True)

    xb = x_ref[...].astype(BF16)
    q_ref[...] = _silu(_dot(xb, win_ref[:, 0:hf]))
    f = lb + (1.0 - lb) * _sigmoid(_dot(xb, win_ref[:, hf:2 * hf]))
    lf_ref[...] = jnp.log2(f)
    k_ref[...] = 1.0 - f
    v_ref[...] = _dot(xb, win_ref[:, 2 * hf:2 * hf + hv]).astype(BF16)
    gt_ref[...] = _silu(_dot(xb, win_ref[:, 2 * hf + hv:2 * hf + 2 * hv]))

    def exponents(ci):
        lf3 = jnp.concatenate(_split3(lf_ref[ci * c:(ci + 1) * c, :]), axis=0)
        e_ref[ci] = _dot(w3_ref[...], lf3)

    def chunk(ci):
        rows = slice(ci * c, (ci + 1) * c)
        e = e_ref.at[ci]

        def operands(h):
            ck = slice(h * dk, (h + 1) * dk)
            qh = q_ref[rows, ck]
            kh = k_ref[rows, ck]
            g = e[0:c, ck]
            pairs = []
            for l in range(coarse):
                m = c >> l
                half = m // 2
                qs, ks_ = [], []
                for p in range(0, c, m):
                    ga = e[p + half - 1:p + half, ck]
                    lo = slice(p, p + half)
                    up = slice(p + half, p + m)
                    zero = jnp.zeros((half, dk), F32)
                    qs += [zero, qh[up] * jnp.exp2(g[up] - ga)]
                    ks_ += [kh[lo] * jnp.exp2(ga - g[lo]), zero]
                pairs.append((jnp.concatenate(qs, axis=0).astype(BF16),
                              jnp.concatenate(ks_, axis=0).astype(BF16)))
            for l in range(coarse, levels):
                p = jnp.exp2(e[(1 + l - coarse) * c:(2 + l - coarse) * c, ck])
                pu = p * um_ref[l]
                pairs.append(((qh * pu).astype(BF16), (kh * (p - pu)).astype(BF16)))
            qi = (qh * jnp.exp2(g)).astype(BF16)
            ks = (kh * jnp.exp2(e[c - 1:c, ck] - g)).astype(BF16)
            return pairs, qi, ks

        def level_scores(h, pairs):
            for l, (ql, kl) in enumerate(pairs):
                sc_ref[h % HG_SLOTS, l] = _dot_nt(ql, kl)

        def finish(h, qi, ks):
            ck = slice(h * dk, (h + 1) * dk)
            cv = slice(h * dv, (h + 1) * dv)
            a = sc_ref[h % HG_SLOTS, 0]
            for l in range(1, levels):
                a = a + sc_ref[h % HG_SLOTS, l] * mask_ref[l]
            vh = v_ref[rows, cv]
            qh = q_ref[rows, ck]
            kh = k_ref[rows, ck]
            vf = vh.astype(F32)
            d0 = jnp.sum(qh * kh, axis=-1, keepdims=True)
            d1 = jnp.sum(qh * (1.0 - kh) * pltpu.roll(kh, 1, 0), axis=-1, keepdims=True)
            odd = lax.broadcasted_iota(jnp.int32, d1.shape, 0) % 2 == 1
            near = d0 * vf + jnp.where(odd, d1, 0.0) * pltpu.roll(vf, 1, 0)
            st = st_ref[h]
            oc_ref[rows, cv] = _dot(a.astype(BF16), vh) + _dot_nt(qi, st.astype(BF16)) + near
            st_ref[h] = st * jnp.exp2(e[c - 1:c, ck]) + _dot_tn(vh, ks)

        ops = {}
        lag = HG_SLOTS - 1
        for step in range(heads + lag):
            if 0 <= step - 1 < heads:
                level_scores(step - 1, ops[step - 1][0])
            if step < heads:
                ops[step] = operands(step)
            if 0 <= step - lag < heads:
                _, qi, ks = ops.pop(step - lag)
                finish(step - lag, qi, ks)

    n_chunks = tb // c
    for ci in range(n_chunks):
        exponents(ci)
    for ci in range(n_chunks):
        chunk(ci)

    ng = ng_ref[...]
    for h in range(heads):
        cv = slice(h * dv, (h + 1) * dv)
        oh = oc_ref[:, cv]
        ms = jnp.mean(oh * oh, axis=-1, keepdims=True)
        oc_ref[:, cv] = oh * lax.rsqrt(ms + RMS_EPS) * ng
    og = (oc_ref[...] * gt_ref[...]).astype(BF16)
    y = alpha * x_ref[...] + _dot(og, wout_ref[...])
    o_ref[...] = _layer_norm(y, lng_ref[...], lnb_ref[...])


def _hgrn2_layer(x3, w_in, lower_bounds, norm_g, w_out, ln_g, ln_b, *, alpha, layer, tb=HG_TILE):
    b, t, d = x3.shape
    dv = norm_g.shape[-1]
    heads = w_out.shape[0] // dv
    hf = lower_bounds.shape[-1]
    dk = hf // heads
    hv = heads * dv
    assert dk == 128 and dv == 128 and t % tb == 0 and tb % HG_CHUNK == 0
    nt = t // tb
    w, um, mask, levels, coarse = _hgrn2_tables(HG_CHUNK)
    w3 = jnp.asarray(np.concatenate([w, w, w], axis=1), BF16)
    nr = lower_bounds.shape[0]
    kern = functools.partial(_hgrn2_kernel, alpha=alpha, layer=layer, heads=heads, dk=dk, dv=dv,
                             levels=levels, coarse=coarse, tb=tb)
    c = HG_CHUNK
    out = pl.pallas_call(
        kern,
        grid=(b, nt),
        in_specs=[
            pl.BlockSpec((tb, d), lambda i, j: (i * nt + j, 0)),
            _resident((d, 2 * hf + 2 * hv)),
            _resident((nr, hf)),
            _resident((1, dv)),
            _resident((hv, d)),
            _resident((1, d)),
            _resident((1, d)),
            _resident(w3.shape),
            _resident(um.shape),
            _resident(mask.shape),
        ],
        out_specs=pl.BlockSpec((tb, d), lambda i, j: (i * nt + j, 0)),
        out_shape=jax.ShapeDtypeStruct((b * t, d), F32),
        scratch_shapes=[
            pltpu.VMEM((tb, hf), F32),
            pltpu.VMEM((tb, hf), F32),
            pltpu.VMEM((tb, hf), F32),
            pltpu.VMEM((tb, hv), BF16),
            pltpu.VMEM((tb, hv), F32),
            pltpu.VMEM((tb, hv), F32),
            pltpu.VMEM((tb // c, w.shape[0], hf), F32),
            pltpu.VMEM((heads, dv, dk), F32),
            pltpu.VMEM((HG_SLOTS, levels, c, c), F32),
        ],
        compiler_params=pltpu.CompilerParams(
            dimension_semantics=("parallel", "arbitrary"), vmem_limit_bytes=VMEM_LIMIT_BYTES),
        name="hgrn2",
    )(x3.reshape(b * t, d), w_in.astype(BF16), lower_bounds.astype(F32), norm_g.reshape(1, dv),
      w_out.astype(BF16), ln_g.reshape(1, d), ln_b.reshape(1, d),
      w3, jnp.asarray(um), jnp.asarray(mask))
    return out.reshape(b, t, d)


def _bias_tables(heads, dh):
    pk = np.zeros((128, heads * 128), np.float32)
    for h in range(heads):
        for piece in range(3):
            pk[piece * heads + h, 128 * h + dh + piece] = -1.0
            pk[3 * heads, 128 * h + dh + 3 + piece] = 1.0
    return pk


def _kv_kernel(h_ref, wk_ref, wv_ref, fgw_ref, fgb_ref, tril_ref, pk_ref, sel_ref,
               k_ref, vt_ref, c_ref, kn_ref, carry_ref, knmax_ref, *, heads, dh, blk):
    @pl.when(pl.program_id(1) == 0)
    def _():
        carry_ref[...] = jnp.zeros_like(carry_ref)
        knmax_ref[...] = jnp.zeros_like(knmax_ref)

    hb = h_ref[...].astype(BF16)
    z = _dot(hb, fgw_ref[...]) + fgb_ref[...]
    lf = jnp.minimum(z, 0.0) - jnp.log(1.0 + jnp.exp(-jnp.abs(z)))
    hi, mid, lo = _split3(lf)
    tril = tril_ref[...]
    cs = _dot(tril, hi) + _dot(tril, mid) + _dot(tril, lo) + carry_ref[...]
    c_ref[...] = cs
    carry_ref[...] = cs[cs.shape[0] - 1:, :]
    chi, cmid, clo = _split3(cs * LOG2E)
    lane = lax.broadcasted_iota(jnp.int32, cs.shape, 1)
    c4 = jnp.where(lane < heads, chi, jnp.where(lane < 2 * heads, cmid, jnp.where(
        lane < 3 * heads, clo, jnp.where(lane == 3 * heads, 1.0, 0.0).astype(BF16))))
    bias = _dot(c4, pk_ref[...])
    ku = _dot(hb, wk_ref[...])
    low = lane < dh
    for p in range(heads // 2):
        pair = ku[:, 128 * p:128 * (p + 1)]
        even = slice(128 * 2 * p, 128 * (2 * p + 1))
        odd = slice(128 * (2 * p + 1), 128 * (2 * p + 2))
        k_ref[:, even] = jnp.where(low, pair, bias[:, even]).astype(BF16)
        k_ref[:, odd] = jnp.where(low, pltpu.roll(pair, 128 - dh, 1), bias[:, odd]).astype(BF16)
    vt = _dot(hb, wv_ref[...]).astype(BF16).T
    pad = FOX_VROWS - dh
    ones_rows = (lax.broadcasted_iota(jnp.int32, (pad, blk), 0) == 0).astype(BF16)
    for jb in range(vt.shape[1] // blk):
        for h in range(heads):
            vt_ref[0, jb, FOX_VROWS * h:FOX_VROWS * h + dh, :] = (
                vt[dh * h:dh * (h + 1), jb * blk:(jb + 1) * blk])
            vt_ref[0, jb, FOX_VROWS * h + dh:FOX_VROWS * (h + 1), :] = ones_rows
    kf = ku.astype(BF16).astype(F32)
    n2 = jnp.max(_dot((kf * kf).astype(BF16), sel_ref[...]), axis=0, keepdims=True)
    knmax_ref[...] = jnp.maximum(knmax_ref[...], n2)
    kn_ref[0] = jnp.broadcast_to(knmax_ref[...], kn_ref.shape[1:])


def _shared_kv(h3, kv_w, fg_w, fg_b, *, blk, tb=KV_TILE):
    b, t, d = h3.shape
    dim = kv_w.shape[1] // 2
    heads = fg_w.shape[1]
    dh = dim // heads
    assert 3 * heads < 128 and dh + 6 <= 128 and dh < FOX_VROWS and 2 * dh == 128 and heads % 2 == 0
    assert dh % 16 == 0 and FOX_VROWS % 16 == 0
    assert t % tb == 0 and tb % blk == 0
    nt = t // tb
    nsub = tb // blk
    fgw = jnp.zeros((d, 128), BF16).at[:, :3 * heads].set(jnp.tile(fg_w.astype(BF16), (1, 3)))
    fgb = jnp.zeros((1, 128), F32).at[0, :3 * heads].set(jnp.tile(fg_b.astype(F32), 3))
    tril = jnp.asarray(np.tril(np.ones((tb, tb), np.float32)), BF16)
    pk = _bias_tables(heads, dh)
    wk = kv_w[:, :dim].astype(BF16)
    wv = kv_w[:, dim:].astype(BF16)
    sel = np.zeros((heads, dh, 128), np.float32)
    for h in range(heads):
        sel[h, :, h] = 1.0
    sel = jnp.asarray(sel.reshape(dim, 128), BF16)
    k, vt, c, kn = pl.pallas_call(
        functools.partial(_kv_kernel, heads=heads, dh=dh, blk=blk),
        grid=(b, nt),
        in_specs=[
            pl.BlockSpec((tb, d), lambda i, j: (i * nt + j, 0)),
            _resident((d, dim)),
            _resident((d, dim)),
            _resident((d, 128)),
            _resident((1, 128)),
            _resident((tb, tb)),
            _resident(pk.shape),
            _resident((dim, 128)),
        ],
        out_specs=[
            pl.BlockSpec((tb, heads * 128), lambda i, j: (i * nt + j, 0)),
            pl.BlockSpec((1, nsub, heads * FOX_VROWS, blk), lambda i, j: (i, j, 0, 0)),
            pl.BlockSpec((tb, 128), lambda i, j: (i * nt + j, 0)),
            pl.BlockSpec((1, SUBLANES, 128), lambda i, j: (i, 0, 0)),
        ],
        out_shape=[
            jax.ShapeDtypeStruct((b * t, heads * 128), BF16),
            jax.ShapeDtypeStruct((b, t // blk, heads * FOX_VROWS, blk), BF16),
            jax.ShapeDtypeStruct((b * t, 128), F32),
            jax.ShapeDtypeStruct((b, SUBLANES, 128), F32),
        ],
        scratch_shapes=[pltpu.VMEM((1, 128), F32), pltpu.VMEM((1, 128), F32)],
        compiler_params=pltpu.CompilerParams(
            dimension_semantics=("parallel", "arbitrary"), vmem_limit_bytes=VMEM_LIMIT_BYTES),
        name="shared_kv",
    )(h3.reshape(b * t, d), wk, wv, fgw, fgb, tril, jnp.asarray(pk, BF16), sel)
    return k.reshape(b, t, heads * 128), vt, c.reshape(b, t, 128), kn[:, 0, :]


def _fox_kernel(x_ref, wqt_ref, selq_ref, k_ref, vt_ref, ct_ref, kn_ref, cend_ref,
                wout_ref, lng_ref, lnb_ref,
                o_ref, qt_ref, m_ref, acc_ref, s_ref, att_ref, *, alpha, heads, dh, tq):
    i = pl.program_id(1)
    vr = FOX_VROWS
    xb = x_ref[...].astype(BF16)
    qt = _dot_nt(wqt_ref[...], xb) * LOG2E
    ctl = ct_ref[0] * LOG2E
    chi, cmid, clo = (p.astype(F32) for p in _split3(ctl))
    r16 = lax.broadcasted_iota(jnp.int32, (2 * SUBLANES, tq), 0)
    zpad = jnp.zeros((128 - dh - 2 * SUBLANES, tq), BF16)
    for h in range(heads):
        bias = jnp.where(r16 < 3, 1.0,
                         jnp.where(r16 == 3, chi[h:h + 1],
                                   jnp.where(r16 == 4, cmid[h:h + 1],
                                             jnp.where(r16 == 5, clo[h:h + 1], 0.0))))
        qt_ref[128 * h:128 * h + dh, :] = qt[dh * h:dh * (h + 1)].astype(BF16)
        qt_ref[128 * h + dh:128 * h + dh + 2 * SUBLANES, :] = bias.astype(BF16)
        qt_ref[128 * h + dh + 2 * SUBLANES:128 * (h + 1), :] = zpad
    m_ref[...] = jnp.full_like(m_ref, NEG_BIG)
    acc_ref[...] = jnp.zeros_like(acc_ref)

    qf = qt.astype(BF16).astype(F32)
    qn2 = jnp.max(_dot(selq_ref[...], (qf * qf).astype(BF16)), axis=1, keepdims=True)
    bound = 2.0 * jnp.sqrt(qn2 * NORM_MARGIN) * jnp.sqrt(kn_ref[0] * NORM_MARGIN)
    val = bound + ctl[:, 0:1] - cend_ref[0] * LOG2E
    hrow = lax.broadcasted_iota(jnp.int32, val.shape, 0)
    worst = jnp.max(jnp.where(hrow < heads, val, NEG_BIG), axis=0, keepdims=True)
    jcol = lax.broadcasted_iota(jnp.int32, worst.shape, 1)
    first = jnp.min(jnp.where((worst >= -FOX_SKIP_LOG2) & (jcol < i), jcol, i))

    def block(j, masked):
        rows = pl.ds(pl.multiple_of(j * tq, tq), tq)
        if masked:
            r = lax.broadcasted_iota(jnp.int32, (tq, tq), 0)
            cidx = lax.broadcasted_iota(jnp.int32, (tq, tq), 1)
            causal = r <= cidx

        def scores(h):
            s = _dot(k_ref[0, rows, 128 * h:128 * (h + 1)], qt_ref[128 * h:128 * (h + 1), :])
            if masked:
                s = jnp.where(causal, s, NEG_BIG)
            s_ref[h % FOX_AHEAD] = s

        def accumulate(h, corr, pv):
            hs = slice(vr * h, vr * (h + 1))
            acc_ref[hs, :] = corr * acc_ref[hs, :] + pv

        for h in range(FOX_AHEAD - 1):
            scores(h)
        pending = None
        for h in range(heads):
            if h + FOX_AHEAD - 1 < heads:
                scores(h + FOX_AHEAD - 1)
            s = s_ref[h % FOX_AHEAD]
            m_old = m_ref[h]
            m_new = jnp.maximum(m_old, jnp.max(s, axis=0, keepdims=True))
            p = jnp.exp2(s - m_new).astype(BF16)
            corr = jnp.exp2(m_old - m_new)
            m_ref[h] = m_new
            pv = _dot(vt_ref[0, j, vr * h:vr * (h + 1), :], p)
            if pending is not None:
                accumulate(*pending)
            pending = (h, corr, pv)
        accumulate(*pending)

    def body(j, carry):
        block(j, False)
        return carry

    lax.fori_loop(first, i, body, 0)
    block(i, True)

    for h in range(heads):
        num = acc_ref[vr * h:vr * h + dh, :]
        den = acc_ref[vr * h + dh:vr * h + dh + 1, :]
        att_ref[dh * h:dh * (h + 1), :] = (num / den).astype(BF16)
    y = alpha * x_ref[...] + _dot_tn(att_ref[...], wout_ref[...])
    o_ref[...] = _layer_norm(y, lng_ref[...], lnb_ref[...])


def _fox_layer(x3, k, vt, c, kn, w_q, w_out, ln_g, ln_b, *, alpha, heads, tq):
    b, t, d = x3.shape
    dim = w_q.shape[1]
    dh = dim // heads
    assert t % tq == 0 and vt.shape == (b, t // tq, heads * FOX_VROWS, tq)
    nt = t // tq
    assert heads <= 128 and nt <= 128 and dh + 2 * SUBLANES <= 128
    wqt = (w_q * dh ** -0.5).astype(BF16).T
    selq = np.zeros((128, heads, dh), np.float32)
    for h in range(heads):
        selq[h, h, :] = 1.0
    selq = jnp.asarray(selq.reshape(128, dim), BF16)
    ct = c.transpose(0, 2, 1)
    kn_rep = jnp.broadcast_to(kn[:, :, None], (b, 128, 128))
    cend = c.reshape(b, nt, tq, 128)[:, :, tq - 1, :].transpose(0, 2, 1)
    cend = jnp.pad(cend, ((0, 0), (0, 0), (0, 128 - nt)))
    kern = functools.partial(_fox_kernel, alpha=alpha, heads=heads, dh=dh, tq=tq)
    out = pl.pallas_call(
        kern,
        grid=(b, nt),
        in_specs=[
            pl.BlockSpec((tq, d), lambda i, j: (i * nt + j, 0)),
            _resident((dim, d)),
            _resident((128, dim)),
            pl.BlockSpec((1, t, heads * 128), lambda i, j: (i, 0, 0)),
            pl.BlockSpec((1, nt, heads * FOX_VROWS, tq), lambda i, j: (i, 0, 0, 0)),
            pl.BlockSpec((1, 128, tq), lambda i, j: (i, 0, j)),
            pl.BlockSpec((1, 128, 128), lambda i, j: (i, 0, 0)),
            pl.BlockSpec((1, 128, 128), lambda i, j: (i, 0, 0)),
            _resident((dim, d)),
            _resident((1, d)),
            _resident((1, d)),
        ],
        out_specs=pl.BlockSpec((tq, d), lambda i, j: (i * nt + j, 0)),
        out_shape=jax.ShapeDtypeStruct((b * t, d), F32),
        scratch_shapes=[
            pltpu.VMEM((heads * 128, tq), BF16),
            pltpu.VMEM((heads, 1, tq), F32),
            pltpu.VMEM((heads * FOX_VROWS, tq), F32),
            pltpu.VMEM((FOX_AHEAD, tq, tq), F32),
            pltpu.VMEM((dim, tq), BF16),
        ],
        compiler_params=pltpu.CompilerParams(
            dimension_semantics=("parallel", "arbitrary"), vmem_limit_bytes=VMEM_LIMIT_BYTES),
        name="fox",
    )(x3.reshape(b * t, d), wqt, selq, k, vt, ct, kn_rep, cend,
      w_out.astype(BF16), ln_g.reshape(1, d), ln_b.reshape(1, d))
    return out.reshape(b, t, d)


def kernel(x, ffn_ln_g, ffn_ln_b, ffn_w_gate_up, ffn_w_down, mix_ln_g, mix_ln_b, hg_w_in, hg_lower_bounds, hg_norm_g, hg_w_out, kv_w, kv_fg_w, kv_fg_b, fox_w_q, fox_w_out):
    b, t, d = x.shape
    depth = ffn_w_gate_up.shape[0]
    n_a = hg_w_in.shape[0]
    alpha = (2.0 * depth) ** 0.25
    fox_heads = kv_fg_w.shape[1]

    wgu_all = ffn_w_gate_up.astype(BF16)
    wd_all = ffn_w_down.astype(BF16)

    def ffn(x3, l, s):
        return _ffn(x3.reshape(b * t, d), wgu_all, wd_all, l, s,
                    ffn_ln_g[l, s], ffn_ln_b[l, s], alpha=alpha).reshape(b, t, d)

    shared = None
    for l in range(depth):
        x = ffn(x, l, 0)
        if l < n_a:
            x = _hgrn2_layer(x, hg_w_in[l], hg_lower_bounds, hg_norm_g[l], hg_w_out[l],
                             mix_ln_g[l], mix_ln_b[l], alpha=alpha, layer=l)
        else:
            k, vt, c, kn = shared
            x = _fox_layer(x, k, vt, c, kn, fox_w_q[l - n_a], fox_w_out[l - n_a],
                           mix_ln_g[l], mix_ln_b[l], alpha=alpha, heads=fox_heads, tq=FOX_BLOCK)
        x = ffn(x, l, 1)
        if l == n_a - 1:
            shared = _shared_kv(x, kv_w, kv_fg_w, kv_fg_b, blk=FOX_BLOCK)
    return x
```
